```python
import jax, jax.numpy as jnp
from jax import lax
import numpy as np

D_MODEL = 1024
BATCH = 8
SEQ = 4096
DEPTH = 2

HEAD_DIM = 64
N_HEADS_FOX = 8
N_HEADS_MOBA = 8
MIX_WIDTH = (N_HEADS_FOX + N_HEADS_MOBA) * HEAD_DIM
FOX_QKV_W = 3 * N_HEADS_FOX * HEAD_DIM
MOBA_QKV_W = 3 * N_HEADS_MOBA * HEAD_DIM
EVEN_IN_W = FOX_QKV_W + N_HEADS_FOX + MOBA_QKV_W + MIX_WIDTH
Q_BLOCK = 128
MOBA_BLOCK = 256
MOBA_TOPK = 3
ROPE_THETA = 500000.0
ROT_DIM = HEAD_DIM // 4
CONV_WIDTH = 31
CONV_CH = D_MODEL
NORM_EPS = 1e-6
LN_EPS = 1e-5

kernel_name = "fox_moba_conformer_hybrid"


def rms_norm(x, g):
    xf = x.astype(jnp.float32)
    y = xf * lax.rsqrt(jnp.mean(xf * xf, axis=-1, keepdims=True) + NORM_EPS)
    return (y * g.astype(jnp.float32)).astype(x.dtype)


def layer_norm(x, g, b):
    xf = x.astype(jnp.float32)
    mu = jnp.mean(xf, axis=-1, keepdims=True)
    xc = xf - mu
    y = xc * lax.rsqrt(jnp.mean(xc * xc, axis=-1, keepdims=True) + LN_EPS)
    return (y * g.astype(jnp.float32) + b.astype(jnp.float32)).astype(x.dtype)


def partial_rope(x, pos):
    half = ROT_DIM // 2
    inv_freq = ROPE_THETA ** (-jnp.arange(half, dtype=jnp.float32) / half)
    ang = pos.astype(jnp.float32)[:, None] * inv_freq[None, :]
    cos = jnp.cos(ang)[None, :, None, :]
    sin = jnp.sin(ang)[None, :, None, :]
    xf = x[..., :ROT_DIM].astype(jnp.float32)
    x1, x2 = xf[..., :half], xf[..., half:]
    rot = jnp.concatenate([x1 * cos - x2 * sin, x2 * cos + x1 * sin], axis=-1).astype(x.dtype)
    return jnp.concatenate([rot, x[..., ROT_DIM:]], axis=-1)


def fox_attention(q, k, v, logf):
    B, H, S, D = q.shape
    nqb = S // Q_BLOCK
    scale = D ** -0.5
    c = jnp.cumsum(logf, axis=-1)
    q_blocks = q.reshape(B, H, nqb, Q_BLOCK, D).transpose(2, 0, 1, 3, 4)
    c_blocks = c.reshape(B, H, nqb, Q_BLOCK).transpose(2, 0, 1, 3)
    k_pos = jnp.arange(S)

    def one_block(args):
        qb, cb, blk = args
        q_pos = blk * Q_BLOCK + jnp.arange(Q_BLOCK)
        logits = jnp.einsum('bhqd,bhkd->bhqk', qb, k).astype(jnp.float32) * scale
        logits = logits + cb[..., None] - c[:, :, None, :]
        logits = jnp.where(k_pos[None, :] <= q_pos[:, None], logits, -jnp.inf)
        p = jax.nn.softmax(logits, axis=-1)
        return jnp.einsum('bhqk,bhkd->bhqd', p.astype(v.dtype), v)

    out = lax.map(one_block, (q_blocks, c_blocks, jnp.arange(nqb)))
    return out.transpose(1, 0, 3, 2, 4).reshape(B, S, H, D)


def moba_attention(q, k, v):
    B, H, S, D = q.shape
    nqb = S // Q_BLOCK
    nblk = -(-S // MOBA_BLOCK)
    s_pad = nblk * MOBA_BLOCK
    scale = D ** -0.5
    pad = ((0, 0), (0, 0), (0, s_pad - S), (0, 0))
    kb = jnp.pad(k, pad).reshape(B, H, nblk, MOBA_BLOCK, D)
    vb = jnp.pad(v, pad).reshape(B, H, nblk, MOBA_BLOCK, D)
    kmean = jnp.mean(kb.astype(jnp.float32), axis=3)
    ksel = min(MOBA_TOPK, nblk)
    h_idx = jnp.arange(H)[:, None, None]
    blk_ids = jnp.arange(nblk)
    in_blk = jnp.arange(MOBA_BLOCK)

    def one_chunk(n):
        b = n // nqb
        q0 = (n % nqb) * Q_BLOCK
        own = q0 // MOBA_BLOCK
        qc = lax.dynamic_slice(q, (b, 0, q0, 0), (1, H, Q_BLOCK, D))[0]
        kb_b = lax.dynamic_index_in_dim(kb, b, 0, keepdims=False)
        vb_b = lax.dynamic_index_in_dim(vb, b, 0, keepdims=False)
        km_b = lax.dynamic_index_in_dim(kmean, b, 0, keepdims=False)
        gate = jnp.einsum('hqd,hnd->hqn', qc.astype(jnp.float32), km_b)
        gate = jnp.where(blk_ids[None, None, :] < own, gate, -jnp.inf)
        _, idx = lax.top_k(gate, ksel)
        valid = idx < own
        kg = kb_b[h_idx, idx]
        vg = vb_b[h_idx, idx]
        lp = jnp.einsum('hqd,hqnld->hqnl', qc, kg).astype(jnp.float32) * scale
        lp = jnp.where(valid[..., None], lp, -jnp.inf).reshape(H, Q_BLOCK, ksel * MOBA_BLOCK)
        k_own = lax.dynamic_index_in_dim(kb_b, own, 1, keepdims=False)
        v_own = lax.dynamic_index_in_dim(vb_b, own, 1, keepdims=False)
        lo = jnp.einsum('hqd,hld->hql', qc, k_own).astype(jnp.float32) * scale
        q_pos = q0 + jnp.arange(Q_BLOCK)
        key_pos = own * MOBA_BLOCK + in_blk
        lo = jnp.where(key_pos[None, :] <= q_pos[:, None], lo, -jnp.inf)
        p = jax.nn.softmax(jnp.concatenate([lp, lo], axis=-1), axis=-1)
        pp = p[..., :ksel * MOBA_BLOCK].reshape(H, Q_BLOCK, ksel, MOBA_BLOCK).astype(v.dtype)
        po = p[..., ksel * MOBA_BLOCK:].astype(v.dtype)
        return (jnp.einsum('hqnl,hqnld->hqd', pp, vg)
                + jnp.einsum('hql,hld->hqd', po, v_own))

    out = lax.map(one_chunk, jnp.arange(B * nqb))
    return out.reshape(B, nqb, H, Q_BLOCK, D).transpose(0, 1, 3, 2, 4).reshape(B, S, H, D)


def fox_moba_layer(x, norm_g, w_in, b_f, qn_fox, kn_fox, qn_moba, kn_moba, w_out):
    B, S, _ = x.shape
    h = rms_norm(x, norm_g)
    proj = h @ w_in
    fox_qkv, f_logit, moba_qkv, gate = jnp.split(
        proj, [FOX_QKV_W, FOX_QKV_W + N_HEADS_FOX, FOX_QKV_W + N_HEADS_FOX + MOBA_QKV_W], axis=-1)
    fox_qkv = fox_qkv.reshape(B, S, 3, N_HEADS_FOX, HEAD_DIM)
    qa = rms_norm(fox_qkv[:, :, 0], qn_fox)
    ka = rms_norm(fox_qkv[:, :, 1], kn_fox)
    va = fox_qkv[:, :, 2]
    logf = jax.nn.log_sigmoid((f_logit + b_f).astype(jnp.float32))
    out_a = fox_attention(qa.transpose(0, 2, 1, 3), ka.transpose(0, 2, 1, 3),
                          va.transpose(0, 2, 1, 3), logf.transpose(0, 2, 1))
    pos = jnp.arange(S)
    moba_qkv = moba_qkv.reshape(B, S, 3, N_HEADS_MOBA, HEAD_DIM)
    qb = partial_rope(rms_norm(moba_qkv[:, :, 0], qn_moba), pos)
    kb = partial_rope(rms_norm(moba_qkv[:, :, 1], kn_moba), pos)
    vb = moba_qkv[:, :, 2]
    out_b = moba_attention(qb.transpose(0, 2, 1, 3), kb.transpose(0, 2, 1, 3),
                           vb.transpose(0, 2, 1, 3))
    mix = jnp.concatenate([out_a, out_b], axis=2).reshape(B, S, MIX_WIDTH)
    return x + (mix * jax.nn.silu(gate)) @ w_out


def conformer_conv_layer(x, norm_g, w_in, conv_w, conv_b, ln_g, ln_b, w_out):
    h = rms_norm(x, norm_g)
    val, glu_gate, z = jnp.split(h @ w_in, 3, axis=-1)
    u = val * jax.nn.sigmoid(glu_gate)
    u = lax.conv_general_dilated(
        u, conv_w, window_strides=(1,), padding=((CONV_WIDTH - 1, 0),),
        dimension_numbers=('NWC', 'WIO', 'NWC'), feature_group_count=CONV_CH) + conv_b
    u = jax.nn.silu(layer_norm(u, ln_g, ln_b)) * jax.nn.silu(z)
    return x + u @ w_out


def setup_inputs(seed: int = 0) -> dict:
    key = jax.random.key(seed)
    ks = jax.random.split(key, 16)
    f32 = jnp.float32
    nrm = lambda k, shape, s: jax.random.normal(k, shape, f32) * s
    return {
        "x": jax.random.normal(ks[0], (BATCH, SEQ, D_MODEL), f32),
        "l0_norm": 1.0 + nrm(ks[1], (D_MODEL,), 0.02),
        "l0_w_in": nrm(ks[2], (D_MODEL, EVEN_IN_W), D_MODEL ** -0.5),
        "l0_b_f": jnp.linspace(2.0, 7.0, N_HEADS_FOX, dtype=f32) + nrm(ks[3], (N_HEADS_FOX,), 0.1),
        "l0_qn_fox": 1.0 + nrm(ks[4], (HEAD_DIM,), 0.02),
        "l0_kn_fox": 1.0 + nrm(ks[5], (HEAD_DIM,), 0.02),
        "l0_qn_moba": 1.0 + nrm(ks[6], (HEAD_DIM,), 0.02),
        "l0_kn_moba": 1.0 + nrm(ks[7], (HEAD_DIM,), 0.02),
        "l0_w_out": nrm(ks[8], (MIX_WIDTH, D_MODEL), MIX_WIDTH ** -0.5),
        "l1_norm": 1.0 + nrm(ks[9], (D_MODEL,), 0.02),
        "l1_w_in": nrm(ks[10], (D_MODEL, 3 * CONV_CH), D_MODEL ** -0.5),
        "l1_conv_w": nrm(ks[11], (CONV_WIDTH, 1, CONV_CH), CONV_WIDTH ** -0.5),
        "l1_conv_b": nrm(ks[12], (CONV_CH,), 0.02),
        "l1_ln_g": 1.0 + nrm(ks[13], (CONV_CH,), 0.02),
        "l1_ln_b": nrm(ks[14], (CONV_CH,), 0.02),
        "l1_w_out": nrm(ks[15], (CONV_CH, D_MODEL), CONV_CH ** -0.5),
    }


def reference(x, l0_norm, l0_w_in, l0_b_f, l0_qn_fox, l0_kn_fox, l0_qn_moba, l0_kn_moba, l0_w_out,
              l1_norm, l1_w_in, l1_conv_w, l1_conv_b, l1_ln_g, l1_ln_b, l1_w_out):
    even_params = [(l0_norm, l0_w_in, l0_b_f, l0_qn_fox, l0_kn_fox, l0_qn_moba, l0_kn_moba, l0_w_out)]
    odd_params = [(l1_norm, l1_w_in, l1_conv_w, l1_conv_b, l1_ln_g, l1_ln_b, l1_w_out)]
    for layer in range(DEPTH):
        if layer % 2 == 0:
            x = fox_moba_layer(x, *even_params[layer // 2])
        else:
            x = conformer_conv_layer(x, *odd_params[layer // 2])
    return x
```

```python
import functools

import jax
import jax.numpy as jnp
from jax import lax
from jax.experimental import pallas as pl
from jax.experimental.pallas import tpu as pltpu

F32 = jnp.float32
BF16 = jnp.bfloat16

D_MODEL = 1024
HEAD_DIM = 64
N_HEADS = 8
GROUP_W = N_HEADS * HEAD_DIM
MOBA_BLOCK = 256
MOBA_TOPK = 3
ROPE_THETA = 500000.0
ROT_DIM = HEAD_DIM // 4
CONV_WIDTH = 31
NORM_EPS = 1e-6
LN_EPS = 1e-5

LANES = 128
HALO = 32
MASK_BIAS = -(2.0 ** 100)
NEG_BIG = -1e30

TM_IN = 256
TQ = 256
TK = 256
TM_L1 = 256
CONV_ROWS = 64

VMEM_LIMIT = 56 * 1024 * 1024


def _dot(a, b):
    return jnp.dot(a, b, preferred_element_type=F32)


def _dot_nt(a, b):
    return lax.dot_general(a, b, (((1,), (1,)), ((), ())), preferred_element_type=F32)


def _split2(v):
    hi = v.astype(BF16)
    lo = (v - hi.astype(F32)).astype(BF16)
    return hi, lo


def _split3(v):
    p1 = v.astype(BF16)
    r1 = v - p1.astype(F32)
    p2 = r1.astype(BF16)
    r2 = r1 - p2.astype(F32)
    p3 = r2.astype(BF16)
    return p1, p2, p3


def _l0_in_kernel(x_ref, g_ref, w_ref, bf_ref, gfq_ref, gfk_ref, gmq_ref, gmk_ref,
                  cos_ref, sin_ref, p_ref, ltri_ref, selq_ref, selk_ref,
                  qf_ref, kf_ref, vf_ref, qx_ref, kx_ref, qm_ref, km_ref, vm_ref,
                  kmean_ref, sg_ref, carry_ref):
    tm = x_ref.shape[1]
    i = pl.program_id(1)

    x = x_ref[0]
    ms = jnp.mean(x * x, axis=-1, keepdims=True)
    h = (x * lax.rsqrt(ms + NORM_EPS) * g_ref[...]).astype(BF16)

    pmat = p_ref[...]

    def head_norm(y, gain):
        outs = []
        for c in range(GROUP_W // 256):
            yc = y[:, c * 256:(c + 1) * 256]
            hi, lo = _split2(yc * yc)
            ssum = _dot(hi, pmat) + _dot(lo, pmat)
            r = lax.rsqrt(ssum * (1.0 / HEAD_DIM) + NORM_EPS)
            outs.append(yc * r * gain[:, c * 256:(c + 1) * 256])
        return jnp.concatenate(outs, axis=1)

    def rope(y):
        cos = cos_ref[...]
        sin = sin_ref[...]
        lane = lax.broadcasted_iota(jnp.int32, (tm, LANES), 1) % HEAD_DIM
        outs = []
        for c in range(GROUP_W // LANES):
            yc = y[:, c * LANES:(c + 1) * LANES]
            sw = jnp.where(lane < ROT_DIM // 2,
                           pltpu.roll(yc, LANES - ROT_DIM // 2, axis=1),
                           pltpu.roll(yc, ROT_DIM // 2, axis=1))
            outs.append(yc * cos + sw * sin)
        return jnp.concatenate(outs, axis=1)

    def proj(col):
        return _dot(h, w_ref[:, col * GROUP_W:(col + 1) * GROUP_W])

    qf_ref[0] = (head_norm(proj(0), gfq_ref[...]) * (HEAD_DIM ** -0.5)).astype(BF16)
    kf_ref[0] = head_norm(proj(1), gfk_ref[...]).astype(BF16)
    vf_ref[0] = proj(2).astype(BF16)

    qm_ref[0] = (rope(head_norm(proj(3), gmq_ref[...])) * (HEAD_DIM ** -0.5)).astype(BF16)
    km = rope(head_norm(proj(4), gmk_ref[...]))
    km_ref[0] = km.astype(BF16)
    for blk in range(tm // MOBA_BLOCK):
        kmean_ref[0, blk] = jnp.sum(km[blk * MOBA_BLOCK:(blk + 1) * MOBA_BLOCK], axis=0,
                                    keepdims=True) * (1.0 / MOBA_BLOCK)
    vm_ref[0] = proj(5).astype(BF16)

    for c in range(2):
        yg = proj(6 + c)
        sg_ref[0, :, c * GROUP_W:(c + 1) * GROUP_W] = yg * jax.nn.sigmoid(yg)

    @pl.when(i == 0)
    def _():
        carry_ref[...] = jnp.zeros_like(carry_ref)

    lane = lax.broadcasted_iota(jnp.int32, (tm, LANES), 1)
    yf = _dot(h, w_ref[:, 8 * GROUP_W:8 * GROUP_W + LANES]) + bf_ref[...]
    lf = jnp.where(lane < N_HEADS, jax.nn.log_sigmoid(yf), 0.0)
    ltri = ltri_ref[...]
    p1, p2, p3 = _split3(lf)
    c = _dot(ltri, p1) + _dot(ltri, p2) + _dot(ltri, p3) + carry_ref[...]
    carry_ref[...] = c[tm - 1:tm, :]
    h1, h2, h3 = _split3(c)
    c3 = (h1.astype(F32) + pltpu.roll(h2.astype(F32), N_HEADS, axis=1)
          + pltpu.roll(h3.astype(F32), 2 * N_HEADS, axis=1)
          + jnp.where(lane == 3 * N_HEADS, 1.0, 0.0)).astype(BF16)
    qx_ref[0] = _dot(c3, selq_ref[...]).astype(BF16)
    kx_ref[0] = _dot(c3, selk_ref[...]).astype(BF16)


def _l0_in(x, g, wcat, bfp, gfq, gfk, gmq, gmk, cos, sin, pmat, ltri, selq, selk):
    B, S, D = x.shape
    tm = TM_IN
    nblk = S // MOBA_BLOCK
    grid = (B, S // tm)
    row = lambda w: pl.BlockSpec((1, tm, w), lambda b, i: (b, i, 0))
    const = lambda a: pl.BlockSpec(a.shape, lambda b, i: (0,) * a.ndim)
    bf_act = lambda w: jax.ShapeDtypeStruct((B, S, w), BF16)
    out_shape = (
        bf_act(GROUP_W), bf_act(GROUP_W), bf_act(GROUP_W),
        bf_act(LANES), bf_act(LANES),
        bf_act(GROUP_W), bf_act(GROUP_W), bf_act(GROUP_W),
        jax.ShapeDtypeStruct((B, nblk, 1, GROUP_W), F32),
        jax.ShapeDtypeStruct((B, S, 2 * GROUP_W), F32),
    )
    out_specs = (
        row(GROUP_W), row(GROUP_W), row(GROUP_W), row(LANES), row(LANES),
        row(GROUP_W), row(GROUP_W), row(GROUP_W),
        pl.BlockSpec((1, tm // MOBA_BLOCK, 1, GROUP_W), lambda b, i: (b, i, 0, 0)),
        row(2 * GROUP_W),
    )
    in_specs = [
        row(D), const(g), const(wcat), const(bfp), const(gfq), const(gfk), const(gmq), const(gmk),
        pl.BlockSpec((tm, LANES), lambda b, i: (i, 0)), pl.BlockSpec((tm, LANES), lambda b, i: (i, 0)),
        const(pmat), const(ltri), const(selq), const(selk),
    ]
    return pl.pallas_call(
        _l0_in_kernel,
        grid=grid, in_specs=in_specs, out_specs=out_specs, out_shape=out_shape,
        scratch_shapes=[pltpu.VMEM((1, LANES), F32)],
        compiler_params=pltpu.CompilerParams(
            dimension_semantics=("parallel", "arbitrary"), vmem_limit_bytes=VMEM_LIMIT),
        name="l0_in",
    )(x, g, wcat, bfp, gfq, gfk, gmq, gmk, cos, sin, pmat, ltri, selq, selk)


def _attn_kernel(*refs, moba):
    if moba:
        q_ref, k_ref, v_ref, sg_ref, kmean_ref, y_ref, k2_ref, v2_ref = refs
    else:
        q_ref, k_ref, v_ref, sg_ref, qx_ref, kx_ref, y_ref, k2_ref, v2_ref = refs
    S = k_ref.shape[1]
    tq = q_ref.shape[1]
    pair = pl.program_id(1)
    i = pl.program_id(2)
    half = LANES // 2
    ext_w = 2 * N_HEADS

    @pl.when(i == 0)
    def _():
        rows = 512
        for r in range(S // rows):
            sl = slice(r * rows, (r + 1) * rows)
            lane = lax.broadcasted_iota(jnp.int32, (rows, LANES), 1)
            k = k_ref[0, sl, :].astype(F32)
            v = v_ref[0, sl, :].astype(F32)
            if moba:
                blk = (lax.broadcasted_iota(jnp.int32, (rows, LANES), 0) + r * rows) // MOBA_BLOCK
            else:
                kx = kx_ref[0, sl, :].astype(F32)
            for hh in range(2):
                mine = (lane >= half) if hh else (lane < half)
                k2_ref[hh, sl, 0:LANES] = jnp.where(mine, k, 0.0).astype(BF16)
                if moba:
                    ext = jnp.where(lane == blk + hh * ext_w, 1.0, 0.0)
                else:
                    ext = jnp.where(lane // ext_w == 2 * pair + hh, kx, 0.0)
                k2_ref[hh, sl, LANES:2 * LANES] = ext.astype(BF16)
                one_lane = 0 if hh else half
                v2_ref[hh, sl, :] = jnp.where(mine, v, jnp.where(lane == one_lane, 1.0, 0.0)).astype(BF16)

    q = q_ref[0]
    if moba:
        km = kmean_ref[0].astype(F32)
        nblk = km.shape[0]
        lane_k = lax.broadcasted_iota(jnp.int32, (nblk, LANES), 1)
        kmt = jnp.concatenate(
            [jnp.where(lane_k < half, km, 0.0), jnp.where(lane_k >= half, km, 0.0),
             jnp.zeros((LANES - 2 * nblk, LANES), F32)], axis=0).astype(BF16)
        st = _dot_nt(kmt, q)[0:2 * nblk]
        n = lax.broadcasted_iota(jnp.int32, (2 * nblk, tq), 0) % nblk
        own = i * (tq // MOBA_BLOCK)
        valid = n < own
        gsc = jnp.where(valid, st, -jnp.inf)
        rank = jnp.zeros((2 * nblk, tq), jnp.int32)
        for m in range(nblk):
            gm = jnp.concatenate(
                [jnp.broadcast_to(gsc[m:m + 1], (nblk, tq)),
                 jnp.broadcast_to(gsc[nblk + m:nblk + m + 1], (nblk, tq))], axis=0)
            beats = (gm > gsc) | ((gm == gsc) & (m < n))
            rank = rank + beats.astype(jnp.int32)
        sel = (valid & (rank < MOBA_TOPK)) | (n == own)
        sbt = jnp.concatenate([jnp.where(sel, 0.0, MASK_BIAS),
                               jnp.zeros((LANES - 2 * nblk, tq), F32)], axis=0)
        qext = sbt.T.astype(BF16)
    else:
        qext = qx_ref[0]
    q2 = jnp.concatenate([q, qext], axis=1)

    row_id = lax.broadcasted_iota(jnp.int32, (tq, TK), 0)
    col_id = lax.broadcasted_iota(jnp.int32, (tq, TK), 1)
    causal = col_id <= row_id

    def step(hh, j, carry, diag):
        m, acc = carry
        ks = pl.ds(pl.multiple_of(j * TK, TK), TK)
        s = _dot_nt(q2, k2_ref[hh, ks, :])
        if diag:
            s = jnp.where(causal, s, NEG_BIG)
        m_new = jnp.maximum(m, jnp.max(s, axis=1, keepdims=True))
        alpha = jnp.exp(m - m_new)
        p = jnp.exp(s - m_new).astype(BF16)
        acc = alpha * acc + _dot(p, v2_ref[hh, ks, :])
        return m_new, acc

    accs = []
    for hh in range(2):
        init = (jnp.full((tq, 1), -jnp.inf, F32), jnp.zeros((tq, LANES), F32))
        carry = lax.fori_loop(0, i, lambda j, c, hh=hh: step(hh, j, c, False), init)
        _, acc = step(hh, i, carry, True)
        accs.append(acc)

    lane = lax.broadcasted_iota(jnp.int32, (tq, LANES), 1)
    l0 = accs[0][:, half:half + 1]
    l1 = accs[1][:, 0:1]
    o = jnp.where(lane < half, accs[0] / l0, accs[1] / l1)
    y_ref[0] = (o * sg_ref[0]).astype(BF16)


def _attention(q, k, v, sg, extra, *, moba):
    B, S, _ = q.shape
    npair = GROUP_W // LANES
    grid = (B, npair, S // TQ)
    sg_off = npair if moba else 0
    qspec = pl.BlockSpec((1, TQ, LANES), lambda b, p, i: (b, i, p))
    kvspec = pl.BlockSpec((1, S, LANES), lambda b, p, i: (b, 0, p))
    sgspec = pl.BlockSpec((1, TQ, LANES), lambda b, p, i: (b, i, p + sg_off))
    if moba:
        (kmean,) = extra
        extra_specs = [pl.BlockSpec((1, kmean.shape[1], LANES), lambda b, p, i: (b, 0, p))]
    else:
        extra_specs = [pl.BlockSpec((1, TQ, LANES), lambda b, p, i: (b, i, 0)),
                       pl.BlockSpec((1, S, LANES), lambda b, p, i: (b, 0, 0))]
    return pl.pallas_call(
        functools.partial(_attn_kernel, moba=moba),
        grid=grid,
        in_specs=[qspec, kvspec, kvspec, sgspec] + extra_specs,
        out_specs=pl.BlockSpec((1, TQ, LANES), lambda b, p, i: (b, i, p)),
        out_shape=jax.ShapeDtypeStruct((B, S, GROUP_W), BF16),
        scratch_shapes=[pltpu.VMEM((2, S, 2 * LANES), BF16), pltpu.VMEM((2, S, LANES), BF16)],
        compiler_params=pltpu.CompilerParams(
            dimension_semantics=("parallel", "parallel", "arbitrary"), vmem_limit_bytes=VMEM_LIMIT),
        name="moba_attn" if moba else "fox_attn",
    )(q, k, v, sg, *extra)


def _l1_kernel(x_ref, yf_ref, ym_ref, wo0_ref, g_ref, w1_ref, cw_ref, cb_ref, lng_ref, lnb_ref,
               wo1_ref, o_ref, ubuf_ref, cbuf_ref):
    tm = x_ref.shape[1]
    C = x_ref.shape[2]
    i = pl.program_id(1)

    y = jnp.concatenate([yf_ref[0], ym_ref[0]], axis=1)
    x1 = x_ref[0] + _dot(y, wo0_ref[...])

    ms = jnp.mean(x1 * x1, axis=-1, keepdims=True)
    h = (x1 * lax.rsqrt(ms + NORM_EPS) * g_ref[...]).astype(BF16)

    @pl.when(i == 0)
    def _():
        ubuf_ref[0:HALO, :] = jnp.zeros((HALO, C), F32)

    val = _dot(h, w1_ref[:, 0:C])
    glu = _dot(h, w1_ref[:, C:2 * C])
    ubuf_ref[HALO:HALO + tm, :] = val * jax.nn.sigmoid(glu)

    base = HALO - (CONV_WIDTH - 1)

    def lane_group(c, _):
        cs = pl.ds(pl.multiple_of(c * LANES, LANES), LANES)
        for r in range(tm // CONV_ROWS):
            acc = jnp.zeros((CONV_ROWS, LANES), F32)
            for j in range(CONV_WIDTH):
                acc = acc + cw_ref[j:j + 1, cs] * ubuf_ref[pl.ds(r * CONV_ROWS + base + j, CONV_ROWS), cs]
            cbuf_ref[r * CONV_ROWS:(r + 1) * CONV_ROWS, cs] = acc
        return 0

    lax.fori_loop(0, C // LANES, lane_group, 0)
    ubuf_ref[0:HALO, :] = ubuf_ref[tm:tm + HALO, :]

    cv = cbuf_ref[...] + cb_ref[...]
    mu = jnp.mean(cv, axis=-1, keepdims=True)
    xc = cv - mu
    var = jnp.mean(xc * xc, axis=-1, keepdims=True)
    yln = xc * lax.rsqrt(var + LN_EPS) * lng_ref[...] + lnb_ref[...]
    z = _dot(h, w1_ref[:, 2 * C:3 * C])
    a = (yln * jax.nn.sigmoid(yln)) * (z * jax.nn.sigmoid(z))
    o_ref[0] = x1 + _dot(a.astype(BF16), wo1_ref[...])


def _l1_conv(x, yf, ym, wo0, g, w1, cw, cb, lng, lnb, wo1):
    B, S, D = x.shape
    tm = TM_L1
    row = lambda w: pl.BlockSpec((1, tm, w), lambda b, i: (b, i, 0))
    const = lambda a: pl.BlockSpec(a.shape, lambda b, i: (0,) * a.ndim)
    return pl.pallas_call(
        _l1_kernel,
        grid=(B, S // tm),
        in_specs=[row(D), row(GROUP_W), row(GROUP_W), const(wo0), const(g), const(w1), const(cw),
                  const(cb), const(lng), const(lnb), const(wo1)],
        out_specs=row(D),
        out_shape=jax.ShapeDtypeStruct((B, S, D), F32),
        scratch_shapes=[pltpu.VMEM((tm + HALO, D), F32), pltpu.VMEM((tm, D), F32)],
        compiler_params=pltpu.CompilerParams(
            dimension_semantics=("parallel", "arbitrary"), vmem_limit_bytes=VMEM_LIMIT),
        name="l1_conv",
    )(x, yf, ym, wo0, g, w1, cw, cb, lng, lnb, wo1)


def _rope_tables(S):
    half = ROT_DIM // 2
    inv_freq = ROPE_THETA ** (-jnp.arange(half, dtype=F32) / half)
    ang = jnp.arange(S).astype(F32)[:, None] * inv_freq[None, :]
    cos, sin = jnp.cos(ang), jnp.sin(ang)
    ones = jnp.ones((S, HEAD_DIM - ROT_DIM), F32)
    cos_h = jnp.concatenate([cos, cos, ones], axis=1)
    sin_h = jnp.concatenate([-sin, sin, 0.0 * ones], axis=1)
    reps = LANES // HEAD_DIM
    return jnp.tile(cos_h, (1, reps)), jnp.tile(sin_h, (1, reps))


def _forget_spread():
    selq = jnp.zeros((LANES, LANES), F32)
    selk = jnp.zeros((LANES, LANES), F32)
    for hd in range(N_HEADS):
        for piece in range(3):
            selq = selq.at[piece * N_HEADS + hd, 2 * N_HEADS * hd + piece].set(1.0)
            selq = selq.at[3 * N_HEADS, 2 * N_HEADS * hd + 3 + piece].set(1.0)
            selk = selk.at[3 * N_HEADS, 2 * N_HEADS * hd + piece].set(1.0)
            selk = selk.at[piece * N_HEADS + hd, 2 * N_HEADS * hd + 3 + piece].set(-1.0)
    return selq.astype(BF16), selk.astype(BF16)


def kernel(x, l0_norm, l0_w_in, l0_b_f, l0_qn_fox, l0_kn_fox, l0_qn_moba, l0_kn_moba, l0_w_out,
           l1_norm, l1_w_in, l1_conv_w, l1_conv_b, l1_ln_g, l1_ln_b, l1_w_out):
    B, S, D = x.shape
    assert D == D_MODEL and S % MOBA_BLOCK == 0 and S // MOBA_BLOCK <= N_HEADS * 2

    fq = 3 * GROUP_W
    w = l0_w_in
    wcat = jnp.concatenate(
        [w[:, :fq], w[:, fq + N_HEADS:2 * fq + N_HEADS], w[:, 2 * fq + N_HEADS:],
         jnp.pad(w[:, fq:fq + N_HEADS], ((0, 0), (0, LANES - N_HEADS)))], axis=1).astype(BF16)
    bfp = jnp.pad(l0_b_f, (0, LANES - N_HEADS)).reshape(1, LANES)
    tile_gain = lambda gain: jnp.tile(gain, N_HEADS).reshape(1, GROUP_W)
    cos, sin = _rope_tables(S)
    pmat = jnp.kron(jnp.eye(256 // HEAD_DIM, dtype=F32), jnp.ones((HEAD_DIM, HEAD_DIM), F32)).astype(BF16)
    ltri = jnp.tril(jnp.ones((TM_IN, TM_IN), F32)).astype(BF16)
    selq, selk = _forget_spread()

    qf, kf, vf, qx, kx, qm, km, vm, kmean, sg = _l0_in(
        x, l0_norm.reshape(1, D), wcat, bfp, tile_gain(l0_qn_fox), tile_gain(l0_kn_fox),
        tile_gain(l0_qn_moba), tile_gain(l0_kn_moba), cos, sin, pmat, ltri, selq, selk)

    yf = _attention(qf, kf, vf, sg, (qx, kx), moba=False)
    ym = _attention(qm, km, vm, sg, (kmean.reshape(B, S // MOBA_BLOCK, GROUP_W),), moba=True)

    cw = jnp.pad(l1_conv_w.reshape(CONV_WIDTH, D), ((0, HALO - CONV_WIDTH), (0, 0)))
    return _l1_conv(
        x, yf, ym, l0_w_out.astype(BF16), l1_norm.reshape(1, D), l1_w_in.astype(BF16), cw,
        l1_conv_b.reshape(1, D), l1_ln_g.reshape(1, D), l1_ln_b.reshape(1, D), l1_w_out.astype(BF16))
```

```python
import functools

import jax
import jax.numpy as jnp
from jax import lax
from jax.experimental import pallas as pl
from jax.experimental.pallas import tpu as pltpu

F32 = jnp.float32
BF16 = jnp.bfloat16

D_MODEL = 1024
HEAD_DIM = 64
N_HEADS = 8
GROUP_W = N_HEADS * HEAD_DIM
MOBA_BLOCK = 256
MOBA_TOPK = 3
ROPE_THETA = 500000.0
ROT_DIM = HEAD_DIM // 4
CONV_WIDTH = 31
NORM_EPS = 1e-6
LN_EPS = 1e-5

LANES = 128
SUBLANES = 8
HALO = 32
MASK_BIAS = -(2.0 ** 100)
NEG_BIG = -1e30

TM_IN = 256
TQ = 512
TK = 512
TM_L1 = 256
CONV_ROWS = 64

VMEM_LIMIT = 56 * 1024 * 1024


def _dot(a, b):
    return jnp.dot(a, b, preferred_element_type=F32)


def _dot_nt(a, b):
    return lax.dot_general(a, b, (((1,), (1,)), ((), ())), preferred_element_type=F32)


def _split2(v):
    hi = v.astype(BF16)
    lo = (v - hi.astype(F32)).astype(BF16)
    return hi, lo


def _split3(v):
    p1 = v.astype(BF16)
    r1 = v - p1.astype(F32)
    p2 = r1.astype(BF16)
    r2 = r1 - p2.astype(F32)
    p3 = r2.astype(BF16)
    return p1, p2, p3


def _l0_in_kernel(x_ref, g_ref, w_ref, bf_ref, gfq_ref, gfk_ref, gmq_ref, gmk_ref,
                  cos_ref, sin_ref, p_ref, ltri_ref, selq_ref, selk_ref,
                  qf_ref, kf_ref, vf_ref, qx_ref, kx_ref, qm_ref, km_ref, vm_ref,
                  kmean_ref, sg_ref, carry_ref):
    tm = x_ref.shape[1]
    i = pl.program_id(1)

    x = x_ref[0]
    ms = jnp.mean(x * x, axis=-1, keepdims=True)
    h = (x * lax.rsqrt(ms + NORM_EPS) * g_ref[...]).astype(BF16)

    pmat = p_ref[...]

    def head_norm(y, gain):
        outs = []
        for c in range(GROUP_W // 256):
            yc = y[:, c * 256:(c + 1) * 256]
            hi, lo = _split2(yc * yc)
            ssum = _dot(hi, pmat) + _dot(lo, pmat)
            r = lax.rsqrt(ssum * (1.0 / HEAD_DIM) + NORM_EPS)
            outs.append(yc * r * gain[:, c * 256:(c + 1) * 256])
        return jnp.concatenate(outs, axis=1)

    def rope(y):
        cos = cos_ref[...]
        sin = sin_ref[...]
        lane = lax.broadcasted_iota(jnp.int32, (tm, LANES), 1) % HEAD_DIM
        outs = []
        for c in range(GROUP_W // LANES):
            yc = y[:, c * LANES:(c + 1) * LANES]
            sw = jnp.where(lane < ROT_DIM // 2,
                           pltpu.roll(yc, LANES - ROT_DIM // 2, axis=1),
                           pltpu.roll(yc, ROT_DIM // 2, axis=1))
            outs.append(yc * cos + sw * sin)
        return jnp.concatenate(outs, axis=1)

    def proj(col):
        return _dot(h, w_ref[:, col * GROUP_W:(col + 1) * GROUP_W])

    qf_ref[0] = (head_norm(proj(0), gfq_ref[...]) * (HEAD_DIM ** -0.5)).astype(BF16)
    kf_ref[0] = head_norm(proj(1), gfk_ref[...]).astype(BF16)
    vf_ref[0] = proj(2).astype(BF16)

    qm_ref[0] = (rope(head_norm(proj(3), gmq_ref[...])) * (HEAD_DIM ** -0.5)).astype(BF16)
    km = rope(head_norm(proj(4), gmk_ref[...]))
    km_ref[0] = km.astype(BF16)
    for blk in range(tm // MOBA_BLOCK):
        kmean_ref[0, blk] = jnp.sum(km[blk * MOBA_BLOCK:(blk + 1) * MOBA_BLOCK], axis=0,
                                    keepdims=True) * (1.0 / MOBA_BLOCK)
    vm_ref[0] = proj(5).astype(BF16)

    for c in range(2):
        yg = proj(6 + c)
        sg_ref[0, :, c * GROUP_W:(c + 1) * GROUP_W] = yg * jax.nn.sigmoid(yg)

    @pl.when(i == 0)
    def _():
        carry_ref[...] = jnp.zeros_like(carry_ref)

    lane = lax.broadcasted_iota(jnp.int32, (tm, LANES), 1)
    yf = _dot(h, w_ref[:, 8 * GROUP_W:8 * GROUP_W + LANES]) + bf_ref[...]
    lf = jnp.where(lane < N_HEADS, jax.nn.log_sigmoid(yf), 0.0)
    ltri = ltri_ref[...]
    p1, p2, p3 = _split3(lf)
    c = _dot(ltri, p1) + _dot(ltri, p2) + _dot(ltri, p3) + carry_ref[...]
    carry_ref[...] = c[tm - 1:tm, :]
    h1, h2, h3 = _split3(c)
    c3 = (h1.astype(F32) + pltpu.roll(h2.astype(F32), N_HEADS, axis=1)
          + pltpu.roll(h3.astype(F32), 2 * N_HEADS, axis=1)
          + jnp.where(lane == 3 * N_HEADS, 1.0, 0.0)).astype(BF16)
    qx_ref[0] = _dot(c3, selq_ref[...]).astype(BF16)
    kx_ref[0] = _dot(c3, selk_ref[...]).astype(BF16)


def _l0_in(x, g, wcat, bfp, gfq, gfk, gmq, gmk, cos, sin, pmat, ltri, selq, selk):
    B, S, D = x.shape
    tm = TM_IN
    nblk = S // MOBA_BLOCK
    grid = (B, S // tm)
    row = lambda w: pl.BlockSpec((1, tm, w), lambda b, i: (b, i, 0))
    const = lambda a: pl.BlockSpec(a.shape, lambda b, i: (0,) * a.ndim)
    bf_act = lambda w: jax.ShapeDtypeStruct((B, S, w), BF16)
    out_shape = (
        bf_act(GROUP_W), bf_act(GROUP_W), bf_act(GROUP_W),
        bf_act(LANES), bf_act(LANES),
        bf_act(GROUP_W), bf_act(GROUP_W), bf_act(GROUP_W),
        jax.ShapeDtypeStruct((B, nblk, 1, GROUP_W), F32),
        jax.ShapeDtypeStruct((B, S, 2 * GROUP_W), F32),
    )
    out_specs = (
        row(GROUP_W), row(GROUP_W), row(GROUP_W), row(LANES), row(LANES),
        row(GROUP_W), row(GROUP_W), row(GROUP_W),
        pl.BlockSpec((1, tm // MOBA_BLOCK, 1, GROUP_W), lambda b, i: (b, i, 0, 0)),
        row(2 * GROUP_W),
    )
    in_specs = [
        row(D), const(g), const(wcat), const(bfp), const(gfq), const(gfk), const(gmq), const(gmk),
        pl.BlockSpec((tm, LANES), lambda b, i: (i, 0)), pl.BlockSpec((tm, LANES), lambda b, i: (i, 0)),
        const(pmat), const(ltri), const(selq), const(selk),
    ]
    return pl.pallas_call(
        _l0_in_kernel,
        grid=grid, in_specs=in_specs, out_specs=out_specs, out_shape=out_shape,
        scratch_shapes=[pltpu.VMEM((1, LANES), F32)],
        compiler_params=pltpu.CompilerParams(
            dimension_semantics=("parallel", "arbitrary"), vmem_limit_bytes=VMEM_LIMIT),
        name="l0_in",
    )(x, g, wcat, bfp, gfq, gfk, gmq, gmk, cos, sin, pmat, ltri, selq, selk)


def _attn_kernel(*refs, moba):
    if moba:
        q_ref, k_ref, v_ref, sg_ref, kmean_ref, y_ref, k2_ref, v2_ref = refs
    else:
        q_ref, k_ref, v_ref, sg_ref, qx_ref, kx_ref, y_ref, k2_ref, v2_ref = refs
    S = k_ref.shape[1]
    tq = q_ref.shape[1]
    pair = pl.program_id(1)
    i = pl.program_id(2)
    half = LANES // 2
    ext_w = 2 * N_HEADS

    @pl.when(i == 0)
    def _():
        rows = 512
        for r in range(S // rows):
            sl = slice(r * rows, (r + 1) * rows)
            lane = lax.broadcasted_iota(jnp.int32, (rows, LANES), 1)
            k = k_ref[0, sl, :].astype(F32)
            v = v_ref[0, sl, :].astype(F32)
            if moba:
                blk = (lax.broadcasted_iota(jnp.int32, (rows, LANES), 0) + r * rows) // MOBA_BLOCK
            else:
                kx = kx_ref[0, sl, :].astype(F32)
            for hh in range(2):
                mine = (lane >= half) if hh else (lane < half)
                k2_ref[hh, sl, 0:LANES] = jnp.where(mine, k, 0.0).astype(BF16)
                if moba:
                    ext = jnp.where(lane == blk + hh * ext_w, 1.0, 0.0)
                else:
                    ext = jnp.where(lane // ext_w == 2 * pair + hh, kx, 0.0)
                k2_ref[hh, sl, LANES:2 * LANES] = ext.astype(BF16)
                one_lane = 0 if hh else half
                v2_ref[hh, sl, :] = jnp.where(mine, v, jnp.where(lane == one_lane, 1.0, 0.0)).astype(BF16)

    q = q_ref[0]
    if moba:
        km = kmean_ref[0].astype(F32)
        nblk = km.shape[0]
        lane_k = lax.broadcasted_iota(jnp.int32, (nblk, LANES), 1)
        kmt = jnp.concatenate(
            [jnp.where(lane_k < half, km, 0.0), jnp.where(lane_k >= half, km, 0.0),
             jnp.zeros((LANES - 2 * nblk, LANES), F32)], axis=0).astype(BF16)
        st = _dot_nt(kmt, q)[0:2 * nblk]
        n = lax.broadcasted_iota(jnp.int32, (2 * nblk, tq), 0) % nblk
        own = i * (tq // MOBA_BLOCK) + lax.broadcasted_iota(jnp.int32, (2 * nblk, tq), 1) // MOBA_BLOCK
        valid = n < own
        gsc = jnp.where(valid, st, -jnp.inf)
        rank = jnp.zeros((2 * nblk, tq), jnp.int32)
        for m in range(nblk):
            gm = jnp.concatenate(
                [jnp.broadcast_to(gsc[m:m + 1], (nblk, tq)),
                 jnp.broadcast_to(gsc[nblk + m:nblk + m + 1], (nblk, tq))], axis=0)
            beats = (gm > gsc) | ((gm == gsc) & (m < n))
            rank = rank + beats.astype(jnp.int32)
        sel = (valid & (rank < MOBA_TOPK)) | (n == own)
        sbt = jnp.concatenate([jnp.where(sel, 0.0, MASK_BIAS),
                               jnp.zeros((LANES - 2 * nblk, tq), F32)], axis=0)
        qext = sbt.T.astype(BF16)
    else:
        qext = qx_ref[0]
    q2 = jnp.concatenate([q, qext], axis=1)

    row_id = lax.broadcasted_iota(jnp.int32, (tq, TK), 0)
    col_id = lax.broadcasted_iota(jnp.int32, (tq, TK), 1)
    causal = col_id <= row_id

    def kv_tile(j):
        return pl.ds(pl.multiple_of(j * TK, TK), TK)

    def logits(j):
        ks = kv_tile(j)
        return tuple(_dot_nt(q2, k2_ref[hh, ks, :]) for hh in range(2))

    def softmax_pv(j, s2, state, diag):
        ks = kv_tile(j)
        out = []
        for hh in range(2):
            m, acc = state[hh]
            s = jnp.where(causal, s2[hh], NEG_BIG) if diag else s2[hh]
            m_new = jnp.maximum(m, jnp.max(s, axis=1, keepdims=True))
            alpha = jnp.exp(m - m_new)
            p = jnp.exp(s - m_new).astype(BF16)
            out.append((m_new, alpha * acc + _dot(p, v2_ref[hh, ks, :])))
        return tuple(out)

    state = tuple((jnp.full((tq, 1), -jnp.inf, F32), jnp.zeros((tq, LANES), F32)) for _ in range(2))
    state = lax.fori_loop(0, i, lambda j, st: softmax_pv(j, logits(j), st, False), state)
    state = softmax_pv(i, logits(i), state, True)
    accs = [state[0][1], state[1][1]]

    lane = lax.broadcasted_iota(jnp.int32, (tq, LANES), 1)
    l0 = accs[0][:, half:half + 1]
    l1 = accs[1][:, 0:1]
    o = jnp.where(lane < half, accs[0] / l0, accs[1] / l1)
    y_ref[0] = (o * sg_ref[0]).astype(BF16)


def _attention(q, k, v, sg, extra, *, moba):
    B, S, _ = q.shape
    npair = GROUP_W // LANES
    grid = (B, npair, S // TQ)
    sg_off = npair if moba else 0
    qspec = pl.BlockSpec((1, TQ, LANES), lambda b, p, i: (b, i, p))
    kvspec = pl.BlockSpec((1, S, LANES), lambda b, p, i: (b, 0, p))
    sgspec = pl.BlockSpec((1, TQ, LANES), lambda b, p, i: (b, i, p + sg_off))
    if moba:
        (kmean,) = extra
        extra_specs = [pl.BlockSpec((1, kmean.shape[1], LANES), lambda b, p, i: (b, 0, p))]
    else:
        extra_specs = [pl.BlockSpec((1, TQ, LANES), lambda b, p, i: (b, i, 0)),
                       pl.BlockSpec((1, S, LANES), lambda b, p, i: (b, 0, 0))]
    return pl.pallas_call(
        functools.partial(_attn_kernel, moba=moba),
        grid=grid,
        in_specs=[qspec, kvspec, kvspec, sgspec] + extra_specs,
        out_specs=pl.BlockSpec((1, TQ, LANES), lambda b, p, i: (b, i, p)),
        out_shape=jax.ShapeDtypeStruct((B, S, GROUP_W), BF16),
        scratch_shapes=[pltpu.VMEM((2, S, 2 * LANES), BF16), pltpu.VMEM((2, S, LANES), BF16)],
        compiler_params=pltpu.CompilerParams(
            dimension_semantics=("parallel", "parallel", "arbitrary"), vmem_limit_bytes=VMEM_LIMIT),
        name="moba_attn" if moba else "fox_attn",
    )(q, k, v, sg, *extra)


def _l1_kernel(x_ref, yf_ref, ym_ref, wo0_ref, g_ref, w1_ref, cw_ref, cb_ref, lng_ref, lnb_ref,
               wo1_ref, o_ref, ubuf_ref, cbuf_ref):
    tm = x_ref.shape[1]
    C = x_ref.shape[2]
    i = pl.program_id(1)

    y = jnp.concatenate([yf_ref[0], ym_ref[0]], axis=1)
    x1 = x_ref[0] + _dot(y, wo0_ref[...])

    ms = jnp.mean(x1 * x1, axis=-1, keepdims=True)
    h = (x1 * lax.rsqrt(ms + NORM_EPS) * g_ref[...]).astype(BF16)

    @pl.when(i == 0)
    def _():
        ubuf_ref[0:HALO, :] = jnp.zeros((HALO, C), F32)

    val = _dot(h, w1_ref[:, 0:C])
    glu = _dot(h, w1_ref[:, C:2 * C])
    ubuf_ref[HALO:HALO + tm, :] = val * jax.nn.sigmoid(glu)

    base = HALO - (CONV_WIDTH - 1)

    def lane_group(c, _):
        cs = pl.ds(pl.multiple_of(c * LANES, LANES), LANES)
        for r in range(tm // CONV_ROWS):
            acc = None
            for phase in range(SUBLANES):
                rows = CONV_ROWS + (SUBLANES if phase else 0)
                part = None
                for j in range(CONV_WIDTH):
                    if (base + j) % SUBLANES != phase:
                        continue
                    start = r * CONV_ROWS + base + j - phase
                    term = cw_ref[j:j + 1, cs] * ubuf_ref[pl.ds(start, rows), cs]
                    part = term if part is None else part + term
                if phase:
                    part = pltpu.roll(part, rows - phase, axis=0)[0:CONV_ROWS]
                acc = part if acc is None else acc + part
            cbuf_ref[r * CONV_ROWS:(r + 1) * CONV_ROWS, cs] = acc
        return 0

    lax.fori_loop(0, C // LANES, lane_group, 0)
    ubuf_ref[0:HALO, :] = ubuf_ref[tm:tm + HALO, :]

    cv = cbuf_ref[...] + cb_ref[...]
    mu = jnp.mean(cv, axis=-1, keepdims=True)
    xc = cv - mu
    var = jnp.mean(xc * xc, axis=-1, keepdims=True)
    yln = xc * lax.rsqrt(var + LN_EPS) * lng_ref[...] + lnb_ref[...]
    z = _dot(h, w1_ref[:, 2 * C:3 * C])
    a = (yln * jax.nn.sigmoid(yln)) * (z * jax.nn.sigmoid(z))
    o_ref[0] = x1 + _dot(a.astype(BF16), wo1_ref[...])


def _l1_conv(x, yf, ym, wo0, g, w1, cw, cb, lng, lnb, wo1):
    B, S, D = x.shape
    tm = TM_L1
    row = lambda w: pl.BlockSpec((1, tm, w), lambda b, i: (b, i, 0))
    const = lambda a: pl.BlockSpec(a.shape, lambda b, i: (0,) * a.ndim)
    return pl.pallas_call(
        _l1_kernel,
        grid=(B, S // tm),
        in_specs=[row(D), row(GROUP_W), row(GROUP_W), const(wo0), const(g), const(w1), const(cw),
                  const(cb), const(lng), const(lnb), const(wo1)],
        out_specs=row(D),
        out_shape=jax.ShapeDtypeStruct((B, S, D), F32),
        scratch_shapes=[pltpu.VMEM((tm + HALO, D), F32), pltpu.VMEM((tm, D), F32)],
        compiler_params=pltpu.CompilerParams(
            dimension_semantics=("parallel", "arbitrary"), vmem_limit_bytes=VMEM_LIMIT),
        name="l1_conv",
    )(x, yf, ym, wo0, g, w1, cw, cb, lng, lnb, wo1)


def _rope_tables(S):
    half = ROT_DIM // 2
    inv_freq = ROPE_THETA ** (-jnp.arange(half, dtype=F32) / half)
    ang = jnp.arange(S).astype(F32)[:, None] * inv_freq[None, :]
    cos, sin = jnp.cos(ang), jnp.sin(ang)
    ones = jnp.ones((S, HEAD_DIM - ROT_DIM), F32)
    cos_h = jnp.concatenate([cos, cos, ones], axis=1)
    sin_h = jnp.concatenate([-sin, sin, 0.0 * ones], axis=1)
    reps = LANES // HEAD_DIM
    return jnp.tile(cos_h, (1, reps)), jnp.tile(sin_h, (1, reps))


def _forget_spread():
    selq = jnp.zeros((LANES, LANES), F32)
    selk = jnp.zeros((LANES, LANES), F32)
    for hd in range(N_HEADS):
        for piece in range(3):
            selq = selq.at[piece * N_HEADS + hd, 2 * N_HEADS * hd + piece].set(1.0)
            selq = selq.at[3 * N_HEADS, 2 * N_HEADS * hd + 3 + piece].set(1.0)
            selk = selk.at[3 * N_HEADS, 2 * N_HEADS * hd + piece].set(1.0)
            selk = selk.at[piece * N_HEADS + hd, 2 * N_HEADS * hd + 3 + piece].set(-1.0)
    return selq.astype(BF16), selk.astype(BF16)


def kernel(x, l0_norm, l0_w_in, l0_b_f, l0_qn_fox, l0_kn_fox, l0_qn_moba, l0_kn_moba, l0_w_out,
           l1_norm, l1_w_in, l1_conv_w, l1_conv_b, l1_ln_g, l1_ln_b, l1_w_out):
    B, S, D = x.shape
    assert D == D_MODEL and S % MOBA_BLOCK == 0 and S // MOBA_BLOCK <= N_HEADS * 2

    fq = 3 * GROUP_W
    w = l0_w_in
    wcat = jnp.concatenate(
        [w[:, :fq], w[:, fq + N_HEADS:2 * fq + N_HEADS], w[:, 2 * fq + N_HEADS:],
         jnp.pad(w[:, fq:fq + N_HEADS], ((0, 0), (0, LANES - N_HEADS)))], axis=1).astype(BF16)
    bfp = jnp.pad(l0_b_f, (0, LANES - N_HEADS)).reshape(1, LANES)
    tile_gain = lambda gain: jnp.tile(gain, N_HEADS).reshape(1, GROUP_W)
    cos, sin = _rope_tables(S)
    pmat = jnp.kron(jnp.eye(256 // HEAD_DIM, dtype=F32), jnp.ones((HEAD_DIM, HEAD_DIM), F32)).astype(BF16)
    ltri = jnp.tril(jnp.ones((TM_IN, TM_IN), F32)).astype(BF16)
    selq, selk = _forget_spread()

    qf, kf, vf, qx, kx, qm, km, vm, kmean, sg = _l0_in(
        x, l0_norm.reshape(1, D), wcat, bfp, tile_gain(l0_qn_fox), tile_gain(l0_kn_fox),
        tile_gain(l0_qn_moba), tile_gain(l0_kn_moba), cos, sin, pmat, ltri, selq, selk)

    yf = _attention(qf, kf, vf, sg, (qx, kx), moba=False)
    ym = _attention(qm, km, vm, sg, (kmean.reshape(B, S // MOBA_BLOCK, GROUP_W),), moba=True)

    cw = jnp.pad(l1_conv_w.reshape(CONV_WIDTH, D), ((0, HALO - CONV_WIDTH), (0, 0)))
    return _l1_conv(
        x, yf, ym, l0_w_out.astype(BF16), l1_norm.reshape(1, D), l1_w_in.astype(BF16), cw,
        l1_conv_b.reshape(1, D), l1_ln_g.reshape(1, D), l1_ln_b.reshape(1, D), l1_w_out.astype(BF16))
```

```python
import functools

import jax
import jax.numpy as jnp
from jax import lax
from jax.experimental import pallas as pl
from jax.experimental.pallas import tpu as pltpu

F32 = jnp.float32
BF16 = jnp.bfloat16

D_MODEL = 1024
HEAD_DIM = 64
N_HEADS = 8
GROUP_W = N_HEADS * HEAD_DIM
MOBA_BLOCK = 256
MOBA_TOPK = 3
ROPE_THETA = 500000.0
ROT_DIM = HEAD_DIM // 4
CONV_WIDTH = 31
NORM_EPS = 1e-6
LN_EPS = 1e-5

LANES = 128
SUBLANES = 8
HALO = 32
MASK_BIAS = -(2.0 ** 100)
SHIFT_LANE = LANES - 1
MAX_FIXED_SHIFT = 40.0
NEG_BIG = -1e30

TM_IN = 256
TQ = 512
TK = 512
TM_L1 = 256
CONV_ROWS = 64

VMEM_LIMIT = 56 * 1024 * 1024


def _dot(a, b):
    return jnp.dot(a, b, preferred_element_type=F32)


def _dot_nt(a, b):
    return lax.dot_general(a, b, (((1,), (1,)), ((), ())), preferred_element_type=F32)


def _split2(v):
    hi = v.astype(BF16)
    lo = (v - hi.astype(F32)).astype(BF16)
    return hi, lo


def _split3(v):
    p1 = v.astype(BF16)
    r1 = v - p1.astype(F32)
    p2 = r1.astype(BF16)
    r2 = r1 - p2.astype(F32)
    p3 = r2.astype(BF16)
    return p1, p2, p3


def _l0_in_kernel(x_ref, g_ref, w_ref, bf_ref, gfq_ref, gfk_ref, gmq_ref, gmk_ref,
                  cos_ref, sin_ref, p_ref, ltri_ref, selq_ref, selk_ref,
                  qf_ref, kf_ref, vf_ref, qx_ref, kx_ref, qm_ref, km_ref, vm_ref,
                  kmean_ref, sg_ref, carry_ref):
    tm = x_ref.shape[1]
    i = pl.program_id(1)

    x = x_ref[0]
    ms = jnp.mean(x * x, axis=-1, keepdims=True)
    h = (x * lax.rsqrt(ms + NORM_EPS) * g_ref[...]).astype(BF16)

    pmat = p_ref[...]

    def head_norm(y, gain):
        outs = []
        for c in range(GROUP_W // 256):
            yc = y[:, c * 256:(c + 1) * 256]
            hi, lo = _split2(yc * yc)
            ssum = _dot(hi, pmat) + _dot(lo, pmat)
            r = lax.rsqrt(ssum * (1.0 / HEAD_DIM) + NORM_EPS)
            outs.append(yc * r * gain[:, c * 256:(c + 1) * 256])
        return jnp.concatenate(outs, axis=1)

    def rope(y):
        cos = cos_ref[...]
        sin = sin_ref[...]
        lane = lax.broadcasted_iota(jnp.int32, (tm, LANES), 1) % HEAD_DIM
        outs = []
        for c in range(GROUP_W // LANES):
            yc = y[:, c * LANES:(c + 1) * LANES]
            sw = jnp.where(lane < ROT_DIM // 2,
                           pltpu.roll(yc, LANES - ROT_DIM // 2, axis=1),
                           pltpu.roll(yc, ROT_DIM // 2, axis=1))
            outs.append(yc * cos + sw * sin)
        return jnp.concatenate(outs, axis=1)

    def proj(col):
        return _dot(h, w_ref[:, col * GROUP_W:(col + 1) * GROUP_W])

    qf_ref[0] = (head_norm(proj(0), gfq_ref[...]) * (HEAD_DIM ** -0.5)).astype(BF16)
    kf_ref[0] = head_norm(proj(1), gfk_ref[...]).astype(BF16)
    vf_ref[0] = proj(2).astype(BF16)

    qm_ref[0] = (rope(head_norm(proj(3), gmq_ref[...])) * (HEAD_DIM ** -0.5)).astype(BF16)
    km = rope(head_norm(proj(4), gmk_ref[...]))
    km_ref[0] = km.astype(BF16)
    for blk in range(tm // MOBA_BLOCK):
        kmean_ref[0, blk] = jnp.sum(km[blk * MOBA_BLOCK:(blk + 1) * MOBA_BLOCK], axis=0,
                                    keepdims=True) * (1.0 / MOBA_BLOCK)
    vm_ref[0] = proj(5).astype(BF16)

    for c in range(2):
        yg = proj(6 + c)
        sg_ref[0, :, c * GROUP_W:(c + 1) * GROUP_W] = yg * jax.nn.sigmoid(yg)

    @pl.when(i == 0)
    def _():
        carry_ref[...] = jnp.zeros_like(carry_ref)

    lane = lax.broadcasted_iota(jnp.int32, (tm, LANES), 1)
    yf = _dot(h, w_ref[:, 8 * GROUP_W:8 * GROUP_W + LANES]) + bf_ref[...]
    lf = jnp.where(lane < N_HEADS, jax.nn.log_sigmoid(yf), 0.0)
    ltri = ltri_ref[...]
    p1, p2, p3 = _split3(lf)
    c = _dot(ltri, p1) + _dot(ltri, p2) + _dot(ltri, p3) + carry_ref[...]
    carry_ref[...] = c[tm - 1:tm, :]
    h1, h2, h3 = _split3(c)
    c3 = (h1.astype(F32) + pltpu.roll(h2.astype(F32), N_HEADS, axis=1)
          + pltpu.roll(h3.astype(F32), 2 * N_HEADS, axis=1)
          + jnp.where(lane == 3 * N_HEADS, 1.0, 0.0)).astype(BF16)
    qx_ref[0] = _dot(c3, selq_ref[...]).astype(BF16)
    kx_ref[0] = _dot(c3, selk_ref[...]).astype(BF16)


def _l0_in(x, g, wcat, bfp, gfq, gfk, gmq, gmk, cos, sin, pmat, ltri, selq, selk):
    B, S, D = x.shape
    tm = TM_IN
    nblk = S // MOBA_BLOCK
    grid = (B, S // tm)
    row = lambda w: pl.BlockSpec((1, tm, w), lambda b, i: (b, i, 0))
    const = lambda a: pl.BlockSpec(a.shape, lambda b, i: (0,) * a.ndim)
    bf_act = lambda w: jax.ShapeDtypeStruct((B, S, w), BF16)
    out_shape = (
        bf_act(GROUP_W), bf_act(GROUP_W), bf_act(GROUP_W),
        bf_act(LANES), bf_act(LANES),
        bf_act(GROUP_W), bf_act(GROUP_W), bf_act(GROUP_W),
        jax.ShapeDtypeStruct((B, nblk, 1, GROUP_W), F32),
        jax.ShapeDtypeStruct((B, S, 2 * GROUP_W), F32),
    )
    out_specs = (
        row(GROUP_W), row(GROUP_W), row(GROUP_W), row(LANES), row(LANES),
        row(GROUP_W), row(GROUP_W), row(GROUP_W),
        pl.BlockSpec((1, tm // MOBA_BLOCK, 1, GROUP_W), lambda b, i: (b, i, 0, 0)),
        row(2 * GROUP_W),
    )
    in_specs = [
        row(D), const(g), const(wcat), const(bfp), const(gfq), const(gfk), const(gmq), const(gmk),
        pl.BlockSpec((tm, LANES), lambda b, i: (i, 0)), pl.BlockSpec((tm, LANES), lambda b, i: (i, 0)),
        const(pmat), const(ltri), const(selq), const(selk),
    ]
    return pl.pallas_call(
        _l0_in_kernel,
        grid=grid, in_specs=in_specs, out_specs=out_specs, out_shape=out_shape,
        scratch_shapes=[pltpu.VMEM((1, LANES), F32)],
        compiler_params=pltpu.CompilerParams(
            dimension_semantics=("parallel", "arbitrary"), vmem_limit_bytes=VMEM_LIMIT),
        name="l0_in",
    )(x, g, wcat, bfp, gfq, gfk, gmq, gmk, cos, sin, pmat, ltri, selq, selk)


def _attn_kernel(*refs, moba, online):
    if moba:
        q_ref, k_ref, v_ref, sg_ref, shift_ref, kmean_ref, y_ref, k2_ref, v2_ref, s_ref = refs
    else:
        q_ref, k_ref, v_ref, sg_ref, shift_ref, qx_ref, kx_ref, y_ref, k2_ref, v2_ref, s_ref = refs
    S = k_ref.shape[1]
    tq = q_ref.shape[1]
    pair = pl.program_id(1)
    i = pl.program_id(2)
    half = LANES // 2
    ext_w = 2 * N_HEADS

    @pl.when(i == 0)
    def _():
        rows = 512
        for r in range(S // rows):
            sl = slice(r * rows, (r + 1) * rows)
            lane = lax.broadcasted_iota(jnp.int32, (rows, LANES), 1)
            k = k_ref[0, sl, :].astype(F32)
            v = v_ref[0, sl, :].astype(F32)
            if moba:
                blk = (lax.broadcasted_iota(jnp.int32, (rows, LANES), 0) + r * rows) // MOBA_BLOCK
            else:
                kx = kx_ref[0, sl, :].astype(F32)
            for hh in range(2):
                mine = (lane >= half) if hh else (lane < half)
                k2_ref[hh, sl, 0:LANES] = jnp.where(mine, k, 0.0).astype(BF16)
                if moba:
                    ext = jnp.where(lane == blk + hh * ext_w, 1.0, 0.0)
                else:
                    ext = jnp.where(lane // ext_w == 2 * pair + hh, kx, 0.0)
                if not online:
                    ext = jnp.where(lane == SHIFT_LANE, 1.0, ext)
                k2_ref[hh, sl, LANES:2 * LANES] = ext.astype(BF16)
                one_lane = 0 if hh else half
                v2_ref[hh, sl, :] = jnp.where(mine, v, jnp.where(lane == one_lane, 1.0, 0.0)).astype(BF16)

    q = q_ref[0]
    if moba:
        km = kmean_ref[0].astype(F32)
        nblk = km.shape[0]
        lane_k = lax.broadcasted_iota(jnp.int32, (nblk, LANES), 1)
        kmt = jnp.concatenate(
            [jnp.where(lane_k < half, km, 0.0), jnp.where(lane_k >= half, km, 0.0),
             jnp.zeros((LANES - 2 * nblk, LANES), F32)], axis=0).astype(BF16)
        st = _dot_nt(kmt, q)[0:2 * nblk]
        n = lax.broadcasted_iota(jnp.int32, (2 * nblk, tq), 0) % nblk
        own = i * (tq // MOBA_BLOCK) + lax.broadcasted_iota(jnp.int32, (2 * nblk, tq), 1) // MOBA_BLOCK
        valid = n < own
        gsc = jnp.where(valid, st, -jnp.inf)
        rank = jnp.zeros((2 * nblk, tq), jnp.int32)
        for m in range(nblk):
            gm = jnp.concatenate(
                [jnp.broadcast_to(gsc[m:m + 1], (nblk, tq)),
                 jnp.broadcast_to(gsc[nblk + m:nblk + m + 1], (nblk, tq))], axis=0)
            beats = (gm > gsc) | ((gm == gsc) & (m < n))
            rank = rank + beats.astype(jnp.int32)
        sel = (valid & (rank < MOBA_TOPK)) | (n == own)
        sbt = jnp.concatenate([jnp.where(sel, 0.0, MASK_BIAS),
                               jnp.zeros((LANES - 2 * nblk, tq), F32)], axis=0)
        qext = sbt.T
    else:
        qext = qx_ref[0].astype(F32)
    if not online:
        qext = qext + shift_ref[...]
    q2 = jnp.concatenate([q, qext.astype(BF16)], axis=1)

    row_id = lax.broadcasted_iota(jnp.int32, (tq, TK), 0)
    col_id = lax.broadcasted_iota(jnp.int32, (tq, TK), 1)
    causal = col_id <= row_id

    def kv_tile(j):
        return pl.ds(pl.multiple_of(j * TK, TK), TK)

    def logits(j, slot):
        ks = kv_tile(j)
        for hh in range(2):
            s_ref[slot, hh] = _dot_nt(q2, k2_ref[hh, ks, :])

    def softmax_pv(j, slot, state, diag):
        ks = kv_tile(j)
        out = []
        for hh in range(2):
            s = s_ref[slot, hh]
            if diag:
                s = jnp.where(causal, s, NEG_BIG)
            if online:
                m, acc = state[hh]
                m_new = jnp.maximum(m, jnp.max(s, axis=1, keepdims=True))
                alpha = jnp.exp(m - m_new)
                p = jnp.exp(s - m_new).astype(BF16)
                out.append((m_new, alpha * acc + _dot(p, v2_ref[hh, ks, :])))
            else:
                out.append(state[hh] + _dot(jnp.exp(s).astype(BF16), v2_ref[hh, ks, :]))
        return tuple(out)

    def two_tiles(jj, state):
        a = 2 * jj
        logits(a + 1, 1)
        state = softmax_pv(a, 0, state, False)
        logits(a + 2, 0)
        return softmax_pv(a + 1, 1, state, False)

    def odd_tail(state):
        logits(i, 1)
        return softmax_pv(i - 1, 0, state, False)

    if online:
        state = tuple((jnp.full((tq, 1), -jnp.inf, F32), jnp.zeros((tq, LANES), F32)) for _ in range(2))
    else:
        state = tuple(jnp.zeros((tq, LANES), F32) for _ in range(2))
    logits(0, 0)
    state = lax.fori_loop(0, lax.shift_right_logical(i, 1), two_tiles, state)
    state = lax.cond((i & 1) == 1, odd_tail, lambda st: st, state)
    state = softmax_pv(i, i & 1, state, True)
    accs = [st[1] for st in state] if online else list(state)

    lane = lax.broadcasted_iota(jnp.int32, (tq, LANES), 1)
    l0 = accs[0][:, half:half + 1]
    l1 = accs[1][:, 0:1]
    o = jnp.where(lane < half, accs[0] / l0, accs[1] / l1)
    y_ref[0] = (o * sg_ref[0]).astype(BF16)


def _attention(q, k, v, sg, shift, extra, *, moba, online):
    B, S, _ = q.shape
    npair = GROUP_W // LANES
    grid = (B, npair, S // TQ)
    sg_off = npair if moba else 0
    qspec = pl.BlockSpec((1, TQ, LANES), lambda b, p, i: (b, i, p))
    kvspec = pl.BlockSpec((1, S, LANES), lambda b, p, i: (b, 0, p))
    sgspec = pl.BlockSpec((1, TQ, LANES), lambda b, p, i: (b, i, p + sg_off))
    if moba:
        (kmean,) = extra
        extra_specs = [pl.BlockSpec((1, kmean.shape[1], LANES), lambda b, p, i: (b, 0, p))]
    else:
        extra_specs = [pl.BlockSpec((1, TQ, LANES), lambda b, p, i: (b, i, 0)),
                       pl.BlockSpec((1, S, LANES), lambda b, p, i: (b, 0, 0))]
    return pl.pallas_call(
        functools.partial(_attn_kernel, moba=moba, online=online),
        grid=grid,
        in_specs=[qspec, kvspec, kvspec, sgspec, pl.BlockSpec((1, LANES), lambda b, p, i: (0, 0))] + extra_specs,
        out_specs=pl.BlockSpec((1, TQ, LANES), lambda b, p, i: (b, i, p)),
        out_shape=jax.ShapeDtypeStruct((B, S, GROUP_W), BF16),
        scratch_shapes=[pltpu.VMEM((2, S, 2 * LANES), BF16), pltpu.VMEM((2, S, LANES), BF16),
                        pltpu.VMEM((2, 2, TQ, TK), F32)],
        compiler_params=pltpu.CompilerParams(
            dimension_semantics=("parallel", "parallel", "arbitrary"), vmem_limit_bytes=VMEM_LIMIT),
        name="moba_attn" if moba else "fox_attn",
    )(q, k, v, sg, shift, *extra)


def _l1_kernel(x_ref, yf_ref, ym_ref, wo0_ref, g_ref, w1_ref, cw_ref, cb_ref, lng_ref, lnb_ref,
               wo1_ref, o_ref, ubuf_ref, cbuf_ref):
    tm = x_ref.shape[1]
    C = x_ref.shape[2]
    i = pl.program_id(1)

    y = jnp.concatenate([yf_ref[0], ym_ref[0]], axis=1)
    x1 = x_ref[0] + _dot(y, wo0_ref[...])

    ms = jnp.mean(x1 * x1, axis=-1, keepdims=True)
    h = (x1 * lax.rsqrt(ms + NORM_EPS) * g_ref[...]).astype(BF16)

    @pl.when(i == 0)
    def _():
        ubuf_ref[0:HALO, :] = jnp.zeros((HALO, C), F32)

    val = _dot(h, w1_ref[:, 0:C])
    glu = _dot(h, w1_ref[:, C:2 * C])
    ubuf_ref[HALO:HALO + tm, :] = val * jax.nn.sigmoid(glu)

    base = HALO - (CONV_WIDTH - 1)

    def lane_group(c, _):
        cs = pl.ds(pl.multiple_of(c * LANES, LANES), LANES)
        for r in range(tm // CONV_ROWS):
            acc = None
            for phase in range(SUBLANES):
                rows = CONV_ROWS + (SUBLANES if phase else 0)
                part = None
                for j in range(CONV_WIDTH):
                    if (base + j) % SUBLANES != phase:
                        continue
                    start = r * CONV_ROWS + base + j - phase
                    term = cw_ref[j:j + 1, cs] * ubuf_ref[pl.ds(start, rows), cs]
                    part = term if part is None else part + term
                if phase:
                    part = pltpu.roll(part, rows - phase, axis=0)[0:CONV_ROWS]
                acc = part if acc is None else acc + part
            cbuf_ref[r * CONV_ROWS:(r + 1) * CONV_ROWS, cs] = acc
        return 0

    lax.fori_loop(0, C // LANES, lane_group, 0)
    ubuf_ref[0:HALO, :] = ubuf_ref[tm:tm + HALO, :]

    cv = cbuf_ref[...] + cb_ref[...]
    mu = jnp.mean(cv, axis=-1, keepdims=True)
    xc = cv - mu
    var = jnp.mean(xc * xc, axis=-1, keepdims=True)
    yln = xc * lax.rsqrt(var + LN_EPS) * lng_ref[...] + lnb_ref[...]
    z = _dot(h, w1_ref[:, 2 * C:3 * C])
    a = (yln * jax.nn.sigmoid(yln)) * (z * jax.nn.sigmoid(z))
    o_ref[0] = x1 + _dot(a.astype(BF16), wo1_ref[...])


def _l1_conv(x, yf, ym, wo0, g, w1, cw, cb, lng, lnb, wo1):
    B, S, D = x.shape
    tm = TM_L1
    row = lambda w: pl.BlockSpec((1, tm, w), lambda b, i: (b, i, 0))
    const = lambda a: pl.BlockSpec(a.shape, lambda b, i: (0,) * a.ndim)
    return pl.pallas_call(
        _l1_kernel,
        grid=(B, S // tm),
        in_specs=[row(D), row(GROUP_W), row(GROUP_W), const(wo0), const(g), const(w1), const(cw),
                  const(cb), const(lng), const(lnb), const(wo1)],
        out_specs=row(D),
        out_shape=jax.ShapeDtypeStruct((B, S, D), F32),
        scratch_shapes=[pltpu.VMEM((tm + HALO, D), F32), pltpu.VMEM((tm, D), F32)],
        compiler_params=pltpu.CompilerParams(
            dimension_semantics=("parallel", "arbitrary"), vmem_limit_bytes=VMEM_LIMIT),
        name="l1_conv",
    )(x, yf, ym, wo0, g, w1, cw, cb, lng, lnb, wo1)


def _rope_tables(S):
    half = ROT_DIM // 2
    inv_freq = ROPE_THETA ** (-jnp.arange(half, dtype=F32) / half)
    ang = jnp.arange(S).astype(F32)[:, None] * inv_freq[None, :]
    cos, sin = jnp.cos(ang), jnp.sin(ang)
    ones = jnp.ones((S, HEAD_DIM - ROT_DIM), F32)
    cos_h = jnp.concatenate([cos, cos, ones], axis=1)
    sin_h = jnp.concatenate([-sin, sin, 0.0 * ones], axis=1)
    reps = LANES // HEAD_DIM
    return jnp.tile(cos_h, (1, reps)), jnp.tile(sin_h, (1, reps))


def _forget_spread():
    selq = jnp.zeros((LANES, LANES), F32)
    selk = jnp.zeros((LANES, LANES), F32)
    for hd in range(N_HEADS):
        for piece in range(3):
            selq = selq.at[piece * N_HEADS + hd, 2 * N_HEADS * hd + piece].set(1.0)
            selq = selq.at[3 * N_HEADS, 2 * N_HEADS * hd + 3 + piece].set(1.0)
            selk = selk.at[3 * N_HEADS, 2 * N_HEADS * hd + piece].set(1.0)
            selk = selk.at[piece * N_HEADS + hd, 2 * N_HEADS * hd + 3 + piece].set(-1.0)
    return selq.astype(BF16), selk.astype(BF16)


def kernel(x, l0_norm, l0_w_in, l0_b_f, l0_qn_fox, l0_kn_fox, l0_qn_moba, l0_kn_moba, l0_w_out,
           l1_norm, l1_w_in, l1_conv_w, l1_conv_b, l1_ln_g, l1_ln_b, l1_w_out):
    B, S, D = x.shape
    assert D == D_MODEL and S % MOBA_BLOCK == 0 and S // MOBA_BLOCK <= N_HEADS * 2

    fq = 3 * GROUP_W
    w = l0_w_in
    wcat = jnp.concatenate(
        [w[:, :fq], w[:, fq + N_HEADS:2 * fq + N_HEADS], w[:, 2 * fq + N_HEADS:],
         jnp.pad(w[:, fq:fq + N_HEADS], ((0, 0), (0, LANES - N_HEADS)))], axis=1).astype(BF16)
    bfp = jnp.pad(l0_b_f, (0, LANES - N_HEADS)).reshape(1, LANES)
    tile_gain = lambda gain: jnp.tile(gain, N_HEADS).reshape(1, GROUP_W)
    cos, sin = _rope_tables(S)
    pmat = jnp.kron(jnp.eye(256 // HEAD_DIM, dtype=F32), jnp.ones((HEAD_DIM, HEAD_DIM), F32)).astype(BF16)
    ltri = jnp.tril(jnp.ones((TM_IN, TM_IN), F32)).astype(BF16)
    selq, selk = _forget_spread()

    qf, kf, vf, qx, kx, qm, km, vm, kmean, sg = _l0_in(
        x, l0_norm.reshape(1, D), wcat, bfp, tile_gain(l0_qn_fox), tile_gain(l0_kn_fox),
        tile_gain(l0_qn_moba), tile_gain(l0_kn_moba), cos, sin, pmat, ltri, selq, selk)

    def attend(q, k, v, gq, gk, extra, moba):
        bound = jnp.ceil(1.02 * (HEAD_DIM ** 0.5) * jnp.max(jnp.abs(gq)) * jnp.max(jnp.abs(gk)))
        shift = jnp.zeros((1, LANES), F32).at[0, SHIFT_LANE].set(-bound)
        run = lambda online: functools.partial(_attention, q, k, v, sg, shift, extra, moba=moba, online=online)
        return lax.cond(bound <= MAX_FIXED_SHIFT, run(False), run(True))

    yf = attend(qf, kf, vf, l0_qn_fox, l0_kn_fox, (qx, kx), False)
    ym = attend(qm, km, vm, l0_qn_moba, l0_kn_moba, (kmean.reshape(B, S // MOBA_BLOCK, GROUP_W),), True)

    cw = jnp.pad(l1_conv_w.reshape(CONV_WIDTH, D), ((0, HALO - CONV_WIDTH), (0, 0)))
    return _l1_conv(
        x, yf, ym, l0_w_out.astype(BF16), l1_norm.reshape(1, D), l1_w_in.astype(BF16), cw,
        l1_conv_b.reshape(1, D), l1_ln_g.reshape(1, D), l1_ln_b.reshape(1, D), l1_w_out.astype(BF16))
```

```python
import functools

import jax
import jax.numpy as jnp
from jax import lax
from jax.experimental import pallas as pl
from jax.experimental.pallas import tpu as pltpu

F32 = jnp.float32
BF16 = jnp.bfloat16

D_MODEL = 1024
HEAD_DIM = 64
N_HEADS = 8
GROUP_W = N_HEADS * HEAD_DIM
MOBA_BLOCK = 256
MOBA_TOPK = 3
ROPE_THETA = 500000.0
ROT_DIM = HEAD_DIM // 4
CONV_WIDTH = 31
NORM_EPS = 1e-6
LN_EPS = 1e-5

LANES = 128
SUBLANES = 8
HALO = 32
MASK_BIAS = -(2.0 ** 100)
SHIFT_LANE = LANES - 1
MAX_FIXED_SHIFT = 40.0
NEG_BIG = -1e30
TINY = 1e-37

TM_IN = 512
TQ = 512
TK = 512
STEPS_PER_TRIP = 6
TM_L1 = 512
L1_ROWS = 256
CONV_ROWS = 128

VMEM_LIMIT = 56 * 1024 * 1024


def _dot(a, b):
    return jnp.dot(a, b, preferred_element_type=F32)


def _dot_nt(a, b):
    return lax.dot_general(a, b, (((1,), (1,)), ((), ())), preferred_element_type=F32)


def _split2(v):
    hi = v.astype(BF16)
    lo = (v - hi.astype(F32)).astype(BF16)
    return hi, lo


def _split3(v):
    p1 = v.astype(BF16)
    r1 = v - p1.astype(F32)
    p2 = r1.astype(BF16)
    r2 = r1 - p2.astype(F32)
    p3 = r2.astype(BF16)
    return p1, p2, p3


def _l0_in_kernel(x_ref, g_ref, w_ref, bf_ref, gfq_ref, gfk_ref, gmq_ref, gmk_ref,
                  cos_ref, sin_ref, p_ref, ltri_ref, selq_ref, selk_ref,
                  qf_ref, kf_ref, vf_ref, qx_ref, kx_ref, qm_ref, km_ref, vm_ref,
                  kmean_ref, sg_ref, carry_ref):
    tm = x_ref.shape[1]
    i = pl.program_id(1)

    x = x_ref[0]
    ms = jnp.mean(x * x, axis=-1, keepdims=True)
    h = (x * lax.rsqrt(ms + NORM_EPS) * g_ref[...]).astype(BF16)

    pmat = p_ref[...]

    def head_norm(y, gain):
        outs = []
        for c in range(GROUP_W // 256):
            yc = y[:, c * 256:(c + 1) * 256]
            hi, lo = _split2(yc * yc)
            ssum = _dot(hi, pmat) + _dot(lo, pmat)
            r = lax.rsqrt(ssum * (1.0 / HEAD_DIM) + NORM_EPS)
            outs.append(yc * r * gain[:, c * 256:(c + 1) * 256])
        return jnp.concatenate(outs, axis=1)

    def rope(y):
        cos = cos_ref[...]
        sin = sin_ref[...]
        lane = lax.broadcasted_iota(jnp.int32, (tm, LANES), 1) % HEAD_DIM
        outs = []
        for c in range(GROUP_W // LANES):
            yc = y[:, c * LANES:(c + 1) * LANES]
            sw = jnp.where(lane < ROT_DIM // 2,
                           pltpu.roll(yc, LANES - ROT_DIM // 2, axis=1),
                           pltpu.roll(yc, ROT_DIM // 2, axis=1))
            outs.append(yc * cos + sw * sin)
        return jnp.concatenate(outs, axis=1)

    def proj(col):
        return _dot(h, w_ref[:, col * GROUP_W:(col + 1) * GROUP_W])

    qf_ref[0] = (head_norm(proj(0), gfq_ref[...]) * (HEAD_DIM ** -0.5)).astype(BF16)
    kf_ref[0] = head_norm(proj(1), gfk_ref[...]).astype(BF16)
    vf_ref[0] = proj(2).astype(BF16)

    qm_ref[0] = (rope(head_norm(proj(3), gmq_ref[...])) * (HEAD_DIM ** -0.5)).astype(BF16)
    km = rope(head_norm(proj(4), gmk_ref[...]))
    km_ref[0] = km.astype(BF16)
    for blk in range(tm // MOBA_BLOCK):
        kmean_ref[0, blk] = jnp.sum(km[blk * MOBA_BLOCK:(blk + 1) * MOBA_BLOCK], axis=0,
                                    keepdims=True) * (1.0 / MOBA_BLOCK)
    vm_ref[0] = proj(5).astype(BF16)

    for c in range(2):
        yg = proj(6 + c)
        sg_ref[0, :, c * GROUP_W:(c + 1) * GROUP_W] = yg * jax.nn.sigmoid(yg)

    @pl.when(i == 0)
    def _():
        carry_ref[...] = jnp.zeros_like(carry_ref)

    lane = lax.broadcasted_iota(jnp.int32, (tm, LANES), 1)
    yf = _dot(h, w_ref[:, 8 * GROUP_W:8 * GROUP_W + LANES]) + bf_ref[...]
    lf = jnp.where(lane < N_HEADS, jax.nn.log_sigmoid(yf), 0.0)
    ltri = ltri_ref[...]
    p1, p2, p3 = _split3(lf)
    c = _dot(ltri, p1) + _dot(ltri, p2) + _dot(ltri, p3) + carry_ref[...]
    carry_ref[...] = c[tm - 1:tm, :]
    h1, h2, h3 = _split3(c)
    c3 = (h1.astype(F32) + pltpu.roll(h2.astype(F32), N_HEADS, axis=1)
          + pltpu.roll(h3.astype(F32), 2 * N_HEADS, axis=1)
          + jnp.where(lane == 3 * N_HEADS, 1.0, 0.0)).astype(BF16)
    qx_ref[0] = _dot(c3, selq_ref[...]).astype(BF16)
    kx_ref[0] = _dot(c3, selk_ref[...]).astype(BF16)


def _l0_in(x, g, wcat, bfp, gfq, gfk, gmq, gmk, cos, sin, pmat, ltri, selq, selk):
    B, S, D = x.shape
    tm = TM_IN
    nblk = S // MOBA_BLOCK
    grid = (B, S // tm)
    row = lambda w: pl.BlockSpec((1, tm, w), lambda b, i: (b, i, 0))
    const = lambda a: pl.BlockSpec(a.shape, lambda b, i: (0,) * a.ndim)
    bf_act = lambda w: jax.ShapeDtypeStruct((B, S, w), BF16)
    out_shape = (
        bf_act(GROUP_W), bf_act(GROUP_W), bf_act(GROUP_W),
        bf_act(LANES), bf_act(LANES),
        bf_act(GROUP_W), bf_act(GROUP_W), bf_act(GROUP_W),
        jax.ShapeDtypeStruct((B, nblk, 1, GROUP_W), F32),
        jax.ShapeDtypeStruct((B, S, 2 * GROUP_W), F32),
    )
    out_specs = (
        row(GROUP_W), row(GROUP_W), row(GROUP_W), row(LANES), row(LANES),
        row(GROUP_W), row(GROUP_W), row(GROUP_W),
        pl.BlockSpec((1, tm // MOBA_BLOCK, 1, GROUP_W), lambda b, i: (b, i, 0, 0)),
        row(2 * GROUP_W),
    )
    in_specs = [
        row(D), const(g), const(wcat), const(bfp), const(gfq), const(gfk), const(gmq), const(gmk),
        pl.BlockSpec((tm, LANES), lambda b, i: (i, 0)), pl.BlockSpec((tm, LANES), lambda b, i: (i, 0)),
        const(pmat), const(ltri), const(selq), const(selk),
    ]
    return pl.pallas_call(
        _l0_in_kernel,
        grid=grid, in_specs=in_specs, out_specs=out_specs, out_shape=out_shape,
        scratch_shapes=[pltpu.VMEM((1, LANES), F32)],
        compiler_params=pltpu.CompilerParams(
            dimension_semantics=("parallel", "arbitrary"), vmem_limit_bytes=VMEM_LIMIT),
        name="l0_in",
    )(x, g, wcat, bfp, gfq, gfk, gmq, gmk, cos, sin, pmat, ltri, selq, selk)


def _attn_kernel(ti_ref, tj_ref, *refs, moba, online):
    if moba:
        q_ref, k_ref, v_ref, sg_ref, shift_ref, kmean_ref, y_ref, k2_ref, v2_ref, q2_ref, s_ref = refs
    else:
        q_ref, k_ref, v_ref, sg_ref, shift_ref, qx_ref, kx_ref, y_ref, k2_ref, v2_ref, q2_ref, s_ref = refs
    S = k_ref.shape[1]
    nq = S // TQ
    nsteps = nq * (nq + 1) // 2
    assert STEPS_PER_TRIP % 2 == 0 and nsteps % STEPS_PER_TRIP == 0
    pair = pl.program_id(1)
    half = LANES // 2
    ext_w = 2 * N_HEADS
    rows = 512

    for r in range(S // rows):
        sl = slice(r * rows, (r + 1) * rows)
        lane = lax.broadcasted_iota(jnp.int32, (rows, LANES), 1)
        k = k_ref[0, sl, :].astype(F32)
        v = v_ref[0, sl, :].astype(F32)
        if moba:
            blk = (lax.broadcasted_iota(jnp.int32, (rows, LANES), 0) + r * rows) // MOBA_BLOCK
        else:
            kx = kx_ref[0, sl, :].astype(F32)
        for hh in range(2):
            mine = (lane >= half) if hh else (lane < half)
            k2_ref[hh, sl, 0:LANES] = jnp.where(mine, k, 0.0).astype(BF16)
            if moba:
                ext = jnp.where(lane == blk + hh * ext_w, 1.0, 0.0)
            else:
                ext = jnp.where(lane // ext_w == 2 * pair + hh, kx, 0.0)
            if not online:
                ext = jnp.where(lane == SHIFT_LANE, 1.0, ext)
            k2_ref[hh, sl, LANES:2 * LANES] = ext.astype(BF16)
            one_lane = 0 if hh else half
            v2_ref[hh, sl, :] = jnp.where(mine, v, jnp.where(lane == one_lane, 1.0, 0.0)).astype(BF16)

    def q_rows(i):
        return pl.ds(pl.multiple_of(i * TQ, TQ), TQ)

    def kv_rows(j):
        return pl.ds(pl.multiple_of(j * TK, TK), TK)

    if moba:
        km = kmean_ref[0].astype(F32)
        nblk = km.shape[0]
        lane_k = lax.broadcasted_iota(jnp.int32, (nblk, LANES), 1)
        kmt = jnp.concatenate(
            [jnp.where(lane_k < half, km, 0.0), jnp.where(lane_k >= half, km, 0.0),
             jnp.zeros((LANES - 2 * nblk, LANES), F32)], axis=0).astype(BF16)

        def select(i, _):
            q = q_ref[0, q_rows(i), :]
            st = _dot_nt(kmt, q)[0:2 * nblk]
            n = lax.broadcasted_iota(jnp.int32, (2 * nblk, TQ), 0) % nblk
            own = i * (TQ // MOBA_BLOCK) + lax.broadcasted_iota(jnp.int32, (2 * nblk, TQ), 1) // MOBA_BLOCK
            valid = n < own
            gsc = jnp.where(valid, st, -jnp.inf)
            rank = jnp.zeros((2 * nblk, TQ), jnp.int32)
            for m in range(nblk):
                gm = jnp.concatenate(
                    [jnp.broadcast_to(gsc[m:m + 1], (nblk, TQ)),
                     jnp.broadcast_to(gsc[nblk + m:nblk + m + 1], (nblk, TQ))], axis=0)
                beats = (gm > gsc) | ((gm == gsc) & (m < n))
                rank = rank + beats.astype(jnp.int32)
            sel = (valid & (rank < MOBA_TOPK)) | (n == own)
            sbt = jnp.concatenate([jnp.where(sel, 0.0, MASK_BIAS),
                                   jnp.zeros((LANES - 2 * nblk, TQ), F32)], axis=0)
            qext = sbt.T
            if not online:
                qext = qext + shift_ref[...]
            q2_ref[q_rows(i), 0:LANES] = q
            q2_ref[q_rows(i), LANES:2 * LANES] = qext.astype(BF16)
            return 0

        lax.fori_loop(0, nq, select, 0)
    else:
        for r in range(S // rows):
            sl = slice(r * rows, (r + 1) * rows)
            qext = qx_ref[0, sl, :]
            if not online:
                qext = (qext.astype(F32) + shift_ref[...]).astype(BF16)
            q2_ref[sl, 0:LANES] = q_ref[0, sl, :]
            q2_ref[sl, LANES:2 * LANES] = qext

    col_minus_row = (lax.broadcasted_iota(jnp.int32, (TQ, TK), 1)
                     - lax.broadcasted_iota(jnp.int32, (TQ, TK), 0))
    lane = lax.broadcasted_iota(jnp.int32, (TQ, LANES), 1)

    def logits(t, slot):
        q2 = q2_ref[q_rows(ti_ref[t + 1]), :]
        ks = kv_rows(tj_ref[t + 1])
        for hh in range(2):
            s_ref[slot, hh] = _dot_nt(q2, k2_ref[hh, ks, :])

    def init_state():
        if online:
            return tuple((jnp.full((TQ, 1), -jnp.inf, F32), jnp.zeros((TQ, LANES), F32)) for _ in range(2))
        return tuple(jnp.zeros((TQ, LANES), F32) for _ in range(2))

    def write_out(state, i):
        accs = [st[1] for st in state] if online else state
        l0 = jnp.maximum(accs[0][:, half:half + 1], TINY)
        l1 = jnp.maximum(accs[1][:, 0:1], TINY)
        o = jnp.where(lane < half, accs[0] / l0, accs[1] / l1)
        y_ref[0, q_rows(i), :] = (o * sg_ref[0, q_rows(i), :]).astype(BF16)

    def step(t, slot, state):
        i_prev = ti_ref[t]
        write_out(state, i_prev)
        closed = i_prev == tj_ref[t]
        state = jax.tree.map(lambda a, b: jnp.where(closed, a, b), init_state(), state)

        i = ti_ref[t + 1]
        j = tj_ref[t + 1]
        keep = col_minus_row <= (i - j) * TK
        ks = kv_rows(j)
        new = []
        for hh in range(2):
            s = jnp.where(keep, s_ref[slot, hh], NEG_BIG)
            if online:
                m, acc = state[hh]
                m_new = jnp.maximum(m, jnp.max(s, axis=1, keepdims=True))
                alpha = jnp.exp(m - m_new)
                p = jnp.exp(s - m_new).astype(BF16)
                new.append((m_new, alpha * acc + _dot(p, v2_ref[hh, ks, :])))
            else:
                new.append(state[hh] + _dot(jnp.exp(s).astype(BF16), v2_ref[hh, ks, :]))
        return tuple(new)

    def trip(tt, state):
        for u in range(STEPS_PER_TRIP):
            t = STEPS_PER_TRIP * tt + u
            logits(t + 1, (u + 1) % 2)
            state = step(t, u % 2, state)
        return state

    logits(0, 0)
    state = lax.fori_loop(0, nsteps // STEPS_PER_TRIP, trip, init_state())
    write_out(state, ti_ref[nsteps])


def _attention(q, k, v, sg, shift, extra, *, moba, online):
    B, S, _ = q.shape
    npair = GROUP_W // LANES
    nq = S // TQ
    order = [(0, 1)] + [(i, j) for i in range(nq) for j in range(i + 1)] + [(0, 0), (0, 0)]
    ti = jnp.asarray([i for i, _ in order], jnp.int32)
    tj = jnp.asarray([j for _, j in order], jnp.int32)
    sg_off = npair if moba else 0
    seq = lambda off: pl.BlockSpec((1, S, LANES), lambda b, p, ti, tj: (b, 0, p + off))
    if moba:
        (kmean,) = extra
        extra_specs = [pl.BlockSpec((1, kmean.shape[1], LANES), lambda b, p, ti, tj: (b, 0, p))]
    else:
        all_heads = pl.BlockSpec((1, S, LANES), lambda b, p, ti, tj: (b, 0, 0))
        extra_specs = [all_heads, all_heads]
    return pl.pallas_call(
        functools.partial(_attn_kernel, moba=moba, online=online),
        grid_spec=pltpu.PrefetchScalarGridSpec(
            num_scalar_prefetch=2,
            grid=(B, npair),
            in_specs=[seq(0), seq(0), seq(0), seq(sg_off),
                      pl.BlockSpec((1, LANES), lambda b, p, ti, tj: (0, 0))] + extra_specs,
            out_specs=seq(0),
            scratch_shapes=[pltpu.VMEM((2, S, 2 * LANES), BF16), pltpu.VMEM((2, S, LANES), BF16),
                            pltpu.VMEM((S, 2 * LANES), BF16), pltpu.VMEM((2, 2, TQ, TK), F32)]),
        out_shape=jax.ShapeDtypeStruct((B, S, GROUP_W), BF16),
        compiler_params=pltpu.CompilerParams(
            dimension_semantics=("parallel", "parallel"), vmem_limit_bytes=VMEM_LIMIT),
        name="moba_attn" if moba else "fox_attn",
    )(ti, tj, q, k, v, sg, shift, *extra)


def _l1_kernel(x_ref, yf_ref, ym_ref, wo0_ref, g_ref, w1_ref, cw_ref, cb_ref, lng_ref, lnb_ref,
               wo1_ref, o_ref, ubuf_ref, cbuf_ref):
    tm = x_ref.shape[1]
    C = x_ref.shape[2]
    i = pl.program_id(1)

    @pl.when(i == 0)
    def _():
        ubuf_ref[0:HALO, :] = jnp.zeros((HALO, C), F32)

    base = HALO - (CONV_WIDTH - 1)

    def conv_rows(r0, cs):
        acc = None
        for phase in range(SUBLANES):
            rows = CONV_ROWS + (SUBLANES if phase else 0)
            part = None
            for j in range(CONV_WIDTH):
                if (base + j) % SUBLANES != phase:
                    continue
                term = cw_ref[j:j + 1, cs] * ubuf_ref[pl.ds(r0 + base + j - phase, rows), cs]
                part = term if part is None else part + term
            if phase:
                part = pltpu.roll(part, rows - phase, axis=0)[0:CONV_ROWS]
            acc = part if acc is None else acc + part
        return acc

    for blk in range(tm // L1_ROWS):
        rs = slice(blk * L1_ROWS, (blk + 1) * L1_ROWS)
        y = jnp.concatenate([yf_ref[0, rs, :], ym_ref[0, rs, :]], axis=1)
        x1 = x_ref[0, rs, :] + _dot(y, wo0_ref[...])

        ms = jnp.mean(x1 * x1, axis=-1, keepdims=True)
        h = (x1 * lax.rsqrt(ms + NORM_EPS) * g_ref[...]).astype(BF16)

        val = _dot(h, w1_ref[:, 0:C])
        glu = _dot(h, w1_ref[:, C:2 * C])
        ubuf_ref[HALO + blk * L1_ROWS:HALO + (blk + 1) * L1_ROWS, :] = val * jax.nn.sigmoid(glu)

        for c in range(C // LANES):
            cs = slice(c * LANES, (c + 1) * LANES)
            for r in range(L1_ROWS // CONV_ROWS):
                r0 = blk * L1_ROWS + r * CONV_ROWS
                cbuf_ref[r0:r0 + CONV_ROWS, cs] = conv_rows(r0, cs)

        cv = cbuf_ref[rs, :] + cb_ref[...]
        mu = jnp.mean(cv, axis=-1, keepdims=True)
        xc = cv - mu
        var = jnp.mean(xc * xc, axis=-1, keepdims=True)
        yln = xc * lax.rsqrt(var + LN_EPS) * lng_ref[...] + lnb_ref[...]
        z = _dot(h, w1_ref[:, 2 * C:3 * C])
        a = (yln * jax.nn.sigmoid(yln)) * (z * jax.nn.sigmoid(z))
        o_ref[0, rs, :] = x1 + _dot(a.astype(BF16), wo1_ref[...])

    ubuf_ref[0:HALO, :] = ubuf_ref[tm:tm + HALO, :]


def _l1_conv(x, yf, ym, wo0, g, w1, cw, cb, lng, lnb, wo1):
    B, S, D = x.shape
    tm = TM_L1
    row = lambda w: pl.BlockSpec((1, tm, w), lambda b, i: (b, i, 0))
    const = lambda a: pl.BlockSpec(a.shape, lambda b, i: (0,) * a.ndim)
    return pl.pallas_call(
        _l1_kernel,
        grid=(B, S // tm),
        in_specs=[row(D), row(GROUP_W), row(GROUP_W), const(wo0), const(g), const(w1), const(cw),
                  const(cb), const(lng), const(lnb), const(wo1)],
        out_specs=row(D),
        out_shape=jax.ShapeDtypeStruct((B, S, D), F32),
        scratch_shapes=[pltpu.VMEM((tm + HALO, D), F32), pltpu.VMEM((tm, D), F32)],
        compiler_params=pltpu.CompilerParams(
            dimension_semantics=("parallel", "arbitrary"), vmem_limit_bytes=VMEM_LIMIT),
        name="l1_conv",
    )(x, yf, ym, wo0, g, w1, cw, cb, lng, lnb, wo1)


def _rope_tables(S):
    half = ROT_DIM // 2
    inv_freq = ROPE_THETA ** (-jnp.arange(half, dtype=F32) / half)
    ang = jnp.arange(S).astype(F32)[:, None] * inv_freq[None, :]
    cos, sin = jnp.cos(ang), jnp.sin(ang)
    ones = jnp.ones((S, HEAD_DIM - ROT_DIM), F32)
    cos_h = jnp.concatenate([cos, cos, ones], axis=1)
    sin_h = jnp.concatenate([-sin, sin, 0.0 * ones], axis=1)
    reps = LANES // HEAD_DIM
    return jnp.tile(cos_h, (1, reps)), jnp.tile(sin_h, (1, reps))


def _forget_spread():
    selq = jnp.zeros((LANES, LANES), F32)
    selk = jnp.zeros((LANES, LANES), F32)
    for hd in range(N_HEADS):
        for piece in range(3):
            selq = selq.at[piece * N_HEADS + hd, 2 * N_HEADS * hd + piece].set(1.0)
            selq = selq.at[3 * N_HEADS, 2 * N_HEADS * hd + 3 + piece].set(1.0)
            selk = selk.at[3 * N_HEADS, 2 * N_HEADS * hd + piece].set(1.0)
            selk = selk.at[piece * N_HEADS + hd, 2 * N_HEADS * hd + 3 + piece].set(-1.0)
    return selq.astype(BF16), selk.astype(BF16)


def kernel(x, l0_norm, l0_w_in, l0_b_f, l0_qn_fox, l0_kn_fox, l0_qn_moba, l0_kn_moba, l0_w_out,
           l1_norm, l1_w_in, l1_conv_w, l1_conv_b, l1_ln_g, l1_ln_b, l1_w_out):
    B, S, D = x.shape
    assert D == D_MODEL and S % MOBA_BLOCK == 0 and S // MOBA_BLOCK <= N_HEADS * 2

    fq = 3 * GROUP_W
    w = l0_w_in
    wcat = jnp.concatenate(
        [w[:, :fq], w[:, fq + N_HEADS:2 * fq + N_HEADS], w[:, 2 * fq + N_HEADS:],
         jnp.pad(w[:, fq:fq + N_HEADS], ((0, 0), (0, LANES - N_HEADS)))], axis=1).astype(BF16)
    bfp = jnp.pad(l0_b_f, (0, LANES - N_HEADS)).reshape(1, LANES)
    tile_gain = lambda gain: jnp.tile(gain, N_HEADS).reshape(1, GROUP_W)
    cos, sin = _rope_tables(S)
    pmat = jnp.kron(jnp.eye(256 // HEAD_DIM, dtype=F32), jnp.ones((HEAD_DIM, HEAD_DIM), F32)).astype(BF16)
    ltri = jnp.tril(jnp.ones((TM_IN, TM_IN), F32)).astype(BF16)
    selq, selk = _forget_spread()

    qf, kf, vf, qx, kx, qm, km, vm, kmean, sg = _l0_in(
        x, l0_norm.reshape(1, D), wcat, bfp, tile_gain(l0_qn_fox), tile_gain(l0_kn_fox),
        tile_gain(l0_qn_moba), tile_gain(l0_kn_moba), cos, sin, pmat, ltri, selq, selk)

    def attend(q, k, v, gq, gk, extra, moba):
        bound = jnp.ceil(1.02 * (HEAD_DIM ** 0.5) * jnp.max(jnp.abs(gq)) * jnp.max(jnp.abs(gk)))
        shift = jnp.zeros((1, LANES), F32).at[0, SHIFT_LANE].set(-bound)
        run = lambda online: functools.partial(_attention, q, k, v, sg, shift, extra, moba=moba, online=online)
        return lax.cond(bound <= MAX_FIXED_SHIFT, run(False), run(True))

    yf = attend(qf, kf, vf, l0_qn_fox, l0_kn_fox, (qx, kx), False)
    ym = attend(qm, km, vm, l0_qn_moba, l0_kn_moba, (kmean.reshape(B, S // MOBA_BLOCK, GROUP_W),), True)

    cw = jnp.pad(l1_conv_w.reshape(CONV_WIDTH, D), ((0, HALO - CONV_WIDTH), (0, 0)))
    return _l1_conv(
        x, yf, ym, l0_w_out.astype(BF16), l1_norm.reshape(1, D), l1_w_in.astype(BF16), cw,
        l1_conv_b.reshape(1, D), l1_ln_g.reshape(1, D), l1_ln_b.reshape(1, D), l1_w_out.astype(BF16))
```

```python
import functools

import jax
import jax.numpy as jnp
from jax import lax
from jax.experimental import pallas as pl
from jax.experimental.pallas import tpu as pltpu

F32 = jnp.float32
BF16 = jnp.bfloat16

D_MODEL = 1024
HEAD_DIM = 64
N_HEADS = 8
GROUP_W = N_HEADS * HEAD_DIM
MOBA_BLOCK = 256
MOBA_TOPK = 3
ROPE_THETA = 500000.0
ROT_DIM = HEAD_DIM // 4
CONV_WIDTH = 31
NORM_EPS = 1e-6
LN_EPS = 1e-5

LANES = 128
SUBLANES = 8
HALO = 32
MASK_BIAS = -(2.0 ** 100)
SHIFT_LANE = LANES - 1
MAX_FIXED_SHIFT = 40.0
NEG_BIG = -1e30
TINY = 1e-37

TM_IN = 512
TQ = 512
TK = 512
SELECT_UNROLL = 4
STEPS_PER_TRIP = 6
TM_L1 = 512
L1_ROWS = 256
CONV_ROWS = 128

VMEM_LIMIT = 56 * 1024 * 1024


def _dot(a, b):
    return jnp.dot(a, b, preferred_element_type=F32)


def _dot_nt(a, b):
    return lax.dot_general(a, b, (((1,), (1,)), ((), ())), preferred_element_type=F32)


def _split2(v):
    hi = v.astype(BF16)
    lo = (v - hi.astype(F32)).astype(BF16)
    return hi, lo


def _split3(v):
    p1 = v.astype(BF16)
    r1 = v - p1.astype(F32)
    p2 = r1.astype(BF16)
    r2 = r1 - p2.astype(F32)
    p3 = r2.astype(BF16)
    return p1, p2, p3


def _l0_in_kernel(x_ref, g_ref, w_ref, bf_ref, gfq_ref, gfk_ref, gmq_ref, gmk_ref,
                  cos_ref, sin_ref, p_ref, ltri_ref, selq_ref, selk_ref,
                  qf_ref, kf_ref, vf_ref, qx_ref, kx_ref, qm_ref, km_ref, vm_ref,
                  kmean_ref, sg_ref, carry_ref):
    tm = x_ref.shape[1]

    @pl.when(pl.program_id(1) == 0)
    def _():
        carry_ref[...] = jnp.zeros_like(carry_ref)

    x = x_ref[0]
    ms = jnp.mean(x * x, axis=-1, keepdims=True)
    h = (x * lax.rsqrt(ms + NORM_EPS) * g_ref[...]).astype(BF16)

    pmat = p_ref[...]

    def head_norm(y, gain):
        outs = []
        for c in range(GROUP_W // 256):
            yc = y[:, c * 256:(c + 1) * 256]
            hi, lo = _split2(yc * yc)
            ssum = _dot(hi, pmat) + _dot(lo, pmat)
            r = lax.rsqrt(ssum * (1.0 / HEAD_DIM) + NORM_EPS)
            outs.append(yc * r * gain[:, c * 256:(c + 1) * 256])
        return jnp.concatenate(outs, axis=1)

    def rope(y):
        cos = cos_ref[...]
        sin = sin_ref[...]
        lane = lax.broadcasted_iota(jnp.int32, (tm, LANES), 1) % HEAD_DIM
        outs = []
        for c in range(GROUP_W // LANES):
            yc = y[:, c * LANES:(c + 1) * LANES]
            sw = jnp.where(lane < ROT_DIM // 2,
                           pltpu.roll(yc, LANES - ROT_DIM // 2, axis=1),
                           pltpu.roll(yc, ROT_DIM // 2, axis=1))
            outs.append(yc * cos + sw * sin)
        return jnp.concatenate(outs, axis=1)

    def proj(col):
        return _dot(h, w_ref[:, col * GROUP_W:(col + 1) * GROUP_W])

    lane = lax.broadcasted_iota(jnp.int32, (tm, LANES), 1)
    yf = _dot(h, w_ref[:, 8 * GROUP_W:8 * GROUP_W + LANES]) + bf_ref[...]
    yq = proj(0)
    lf = jnp.where(lane < N_HEADS, jax.nn.log_sigmoid(yf), 0.0)
    ltri = ltri_ref[...]
    p1, p2, p3 = _split3(lf)
    yk = proj(1)
    c = _dot(ltri, p1) + _dot(ltri, p2) + _dot(ltri, p3) + carry_ref[...]
    carry_ref[...] = c[tm - 1:tm, :]
    qf_ref[0] = (head_norm(yq, gfq_ref[...]) * (HEAD_DIM ** -0.5)).astype(BF16)
    h1, h2, h3 = _split3(c)
    c3 = (h1.astype(F32) + pltpu.roll(h2.astype(F32), N_HEADS, axis=1)
          + pltpu.roll(h3.astype(F32), 2 * N_HEADS, axis=1)
          + jnp.where(lane == 3 * N_HEADS, 1.0, 0.0)).astype(BF16)
    kf_ref[0] = head_norm(yk, gfk_ref[...]).astype(BF16)
    vf_ref[0] = proj(2).astype(BF16)
    qx_ref[0] = _dot(c3, selq_ref[...]).astype(BF16)
    kx_ref[0] = _dot(c3, selk_ref[...]).astype(BF16)

    qm_ref[0] = (rope(head_norm(proj(3), gmq_ref[...])) * (HEAD_DIM ** -0.5)).astype(BF16)
    km = rope(head_norm(proj(4), gmk_ref[...]))
    km_ref[0] = km.astype(BF16)
    for blk in range(tm // MOBA_BLOCK):
        kmean_ref[0, blk] = jnp.sum(km[blk * MOBA_BLOCK:(blk + 1) * MOBA_BLOCK], axis=0,
                                    keepdims=True) * (1.0 / MOBA_BLOCK)
    vm_ref[0] = proj(5).astype(BF16)

    for c in range(2):
        yg = proj(6 + c)
        sg_ref[0, :, c * GROUP_W:(c + 1) * GROUP_W] = yg * jax.nn.sigmoid(yg)


def _l0_in(x, g, wcat, bfp, gfq, gfk, gmq, gmk, cos, sin, pmat, ltri, selq, selk):
    B, S, D = x.shape
    tm = TM_IN
    nblk = S // MOBA_BLOCK
    grid = (B, S // tm)
    row = lambda w: pl.BlockSpec((1, tm, w), lambda b, i: (b, i, 0))
    const = lambda a: pl.BlockSpec(a.shape, lambda b, i: (0,) * a.ndim)
    bf_act = lambda w: jax.ShapeDtypeStruct((B, S, w), BF16)
    out_shape = (
        bf_act(GROUP_W), bf_act(GROUP_W), bf_act(GROUP_W),
        bf_act(LANES), bf_act(LANES),
        bf_act(GROUP_W), bf_act(GROUP_W), bf_act(GROUP_W),
        jax.ShapeDtypeStruct((B, nblk, 1, GROUP_W), F32),
        jax.ShapeDtypeStruct((B, S, 2 * GROUP_W), F32),
    )
    out_specs = (
        row(GROUP_W), row(GROUP_W), row(GROUP_W), row(LANES), row(LANES),
        row(GROUP_W), row(GROUP_W), row(GROUP_W),
        pl.BlockSpec((1, tm // MOBA_BLOCK, 1, GROUP_W), lambda b, i: (b, i, 0, 0)),
        row(2 * GROUP_W),
    )
    in_specs = [
        row(D), const(g), const(wcat), const(bfp), const(gfq), const(gfk), const(gmq), const(gmk),
        pl.BlockSpec((tm, LANES), lambda b, i: (i, 0)), pl.BlockSpec((tm, LANES), lambda b, i: (i, 0)),
        const(pmat), const(ltri), const(selq), const(selk),
    ]
    return pl.pallas_call(
        _l0_in_kernel,
        grid=grid, in_specs=in_specs, out_specs=out_specs, out_shape=out_shape,
        scratch_shapes=[pltpu.VMEM((1, LANES), F32)],
        compiler_params=pltpu.CompilerParams(
            dimension_semantics=("parallel", "arbitrary"), vmem_limit_bytes=VMEM_LIMIT),
        name="l0_in",
    )(x, g, wcat, bfp, gfq, gfk, gmq, gmk, cos, sin, pmat, ltri, selq, selk)


def _attn_kernel(ti_ref, tj_ref, *refs, moba, online):
    if moba:
        q_ref, k_ref, v_ref, sg_ref, shift_ref, kmean_ref, y_ref, k2_ref, v2_ref, q2_ref, s_ref = refs
    else:
        q_ref, k_ref, v_ref, sg_ref, shift_ref, qx_ref, kx_ref, y_ref, k2_ref, v2_ref, q2_ref, s_ref = refs
    S = k_ref.shape[1]
    nq = S // TQ
    nsteps = nq * (nq + 1) // 2
    assert STEPS_PER_TRIP % 2 == 0 and nsteps % STEPS_PER_TRIP == 0
    pair = pl.program_id(1)
    half = LANES // 2
    ext_w = 2 * N_HEADS
    rows = 512

    zero = jnp.zeros((rows, LANES), BF16)
    one = jnp.ones((rows, LANES), BF16)
    lane = lax.broadcasted_iota(jnp.int32, (rows, LANES), 1)
    lane_b = lane.astype(BF16)
    group_b = (lane // ext_w).astype(BF16)
    for r in range(S // rows):
        sl = slice(r * rows, (r + 1) * rows)
        k = k_ref[0, sl, :]
        v = v_ref[0, sl, :]
        if moba:
            blk_b = ((lax.broadcasted_iota(jnp.int32, (rows, LANES), 0) + r * rows) // MOBA_BLOCK).astype(BF16)
        else:
            kx = kx_ref[0, sl, :]
        for hh in range(2):
            mine = (lane_b >= half) if hh else (lane_b < half)
            k2_ref[hh, sl, 0:LANES] = jnp.where(mine, k, zero)
            if moba:
                ext = jnp.where(lane_b == blk_b + hh * ext_w, one, zero)
            else:
                head_b = jnp.full((rows, LANES), 2 * pair + hh, jnp.int32).astype(BF16)
                ext = jnp.where(group_b == head_b, kx, zero)
            if not online:
                ext = jnp.where(lane_b == SHIFT_LANE, one, ext)
            k2_ref[hh, sl, LANES:2 * LANES] = ext
            v2_ref[hh, sl, :] = jnp.where(mine, v, jnp.where(lane_b == (0 if hh else half), one, zero))

    def q_rows(i):
        return pl.ds(pl.multiple_of(i * TQ, TQ), TQ)

    def kv_rows(j):
        return pl.ds(pl.multiple_of(j * TK, TK), TK)

    if moba:
        km = kmean_ref[0].astype(F32)
        nblk = km.shape[0]
        lane_k = lax.broadcasted_iota(jnp.int32, (nblk, LANES), 1)
        kmt = jnp.concatenate(
            [jnp.where(lane_k < half, km, 0.0), jnp.where(lane_k >= half, km, 0.0),
             jnp.zeros((LANES - 2 * nblk, LANES), F32)], axis=0).astype(BF16)

        def select(i):
            q = q_ref[0, q_rows(i), :]
            st = _dot_nt(kmt, q)[0:2 * nblk]
            n = lax.broadcasted_iota(jnp.int32, (2 * nblk, TQ), 0) % nblk
            own = i * (TQ // MOBA_BLOCK) + lax.broadcasted_iota(jnp.int32, (2 * nblk, TQ), 1) // MOBA_BLOCK
            valid = n < own
            gsc = jnp.where(valid, st, -jnp.inf)
            rank = jnp.zeros((2 * nblk, TQ), jnp.int32)
            for m in range(nblk):
                gm = jnp.concatenate(
                    [jnp.broadcast_to(gsc[m:m + 1], (nblk, TQ)),
                     jnp.broadcast_to(gsc[nblk + m:nblk + m + 1], (nblk, TQ))], axis=0)
                beats = (gm > gsc) | ((gm == gsc) & (m < n))
                rank = rank + beats.astype(jnp.int32)
            sel = (valid & (rank < MOBA_TOPK)) | (n == own)
            sbt = jnp.concatenate([jnp.where(sel, 0.0, MASK_BIAS),
                                   jnp.zeros((LANES - 2 * nblk, TQ), F32)], axis=0)
            qext = sbt.T
            if not online:
                qext = qext + shift_ref[...]
            q2_ref[q_rows(i), 0:LANES] = q
            q2_ref[q_rows(i), LANES:2 * LANES] = qext.astype(BF16)

        def select_some(ii, _):
            for u in range(SELECT_UNROLL):
                select(SELECT_UNROLL * ii + u)
            return 0

        assert nq % SELECT_UNROLL == 0
        lax.fori_loop(0, nq // SELECT_UNROLL, select_some, 0)
    else:
        shift_b = jnp.broadcast_to(shift_ref[...], (rows, LANES)).astype(BF16)
        for r in range(S // rows):
            sl = slice(r * rows, (r + 1) * rows)
            qext = qx_ref[0, sl, :]
            if not online:
                qext = qext + shift_b
            q2_ref[sl, 0:LANES] = q_ref[0, sl, :]
            q2_ref[sl, LANES:2 * LANES] = qext

    col_minus_row = (lax.broadcasted_iota(jnp.int32, (TQ, TK), 1)
                     - lax.broadcasted_iota(jnp.int32, (TQ, TK), 0))
    lane = lax.broadcasted_iota(jnp.int32, (TQ, LANES), 1)

    def logits(t, slot):
        q2 = q2_ref[q_rows(ti_ref[t + 1]), :]
        ks = kv_rows(tj_ref[t + 1])
        for hh in range(2):
            s_ref[slot, hh] = _dot_nt(q2, k2_ref[hh, ks, :])

    def init_state():
        if online:
            return tuple((jnp.full((TQ, 1), -jnp.inf, F32), jnp.zeros((TQ, LANES), F32)) for _ in range(2))
        return tuple(jnp.zeros((TQ, LANES), F32) for _ in range(2))

    def write_out(state, i):
        accs = [st[1] for st in state] if online else state
        l0 = jnp.maximum(accs[0][:, half:half + 1], TINY)
        l1 = jnp.maximum(accs[1][:, 0:1], TINY)
        o = jnp.where(lane < half, accs[0] / l0, accs[1] / l1)
        y_ref[0, q_rows(i), :] = (o * sg_ref[0, q_rows(i), :]).astype(BF16)

    def step(t, slot, state):
        i_prev = ti_ref[t]
        write_out(state, i_prev)
        closed = i_prev == tj_ref[t]
        state = jax.tree.map(lambda a, b: jnp.where(closed, a, b), init_state(), state)

        i = ti_ref[t + 1]
        j = tj_ref[t + 1]
        keep = col_minus_row <= (i - j) * TK
        ks = kv_rows(j)
        new = []
        for hh in range(2):
            s = jnp.where(keep, s_ref[slot, hh], NEG_BIG)
            if online:
                m, acc = state[hh]
                m_new = jnp.maximum(m, jnp.max(s, axis=1, keepdims=True))
                alpha = jnp.exp(m - m_new)
                p = jnp.exp(s - m_new).astype(BF16)
                new.append((m_new, alpha * acc + _dot(p, v2_ref[hh, ks, :])))
            else:
                new.append(state[hh] + _dot(jnp.exp(s).astype(BF16), v2_ref[hh, ks, :]))
        return tuple(new)

    def trip(tt, state):
        for u in range(STEPS_PER_TRIP):
            t = STEPS_PER_TRIP * tt + u
            logits(t + 1, (u + 1) % 2)
            state = step(t, u % 2, state)
        return state

    logits(0, 0)
    state = lax.fori_loop(0, nsteps // STEPS_PER_TRIP, trip, init_state())
    write_out(state, ti_ref[nsteps])


def _attention(q, k, v, sg, shift, extra, *, moba, online):
    B, S, _ = q.shape
    npair = GROUP_W // LANES
    nq = S // TQ
    order = [(0, 1)] + [(i, j) for i in range(nq) for j in range(i + 1)] + [(0, 0), (0, 0)]
    ti = jnp.asarray([i for i, _ in order], jnp.int32)
    tj = jnp.asarray([j for _, j in order], jnp.int32)
    sg_off = npair if moba else 0
    seq = lambda off: pl.BlockSpec((1, S, LANES), lambda b, p, ti, tj: (b, 0, p + off))
    if moba:
        (kmean,) = extra
        extra_specs = [pl.BlockSpec((1, kmean.shape[1], LANES), lambda b, p, ti, tj: (b, 0, p))]
    else:
        all_heads = pl.BlockSpec((1, S, LANES), lambda b, p, ti, tj: (b, 0, 0))
        extra_specs = [all_heads, all_heads]
    return pl.pallas_call(
        functools.partial(_attn_kernel, moba=moba, online=online),
        grid_spec=pltpu.PrefetchScalarGridSpec(
            num_scalar_prefetch=2,
            grid=(B, npair),
            in_specs=[seq(0), seq(0), seq(0), seq(sg_off),
                      pl.BlockSpec((1, LANES), lambda b, p, ti, tj: (0, 0))] + extra_specs,
            out_specs=seq(0),
            scratch_shapes=[pltpu.VMEM((2, S, 2 * LANES), BF16), pltpu.VMEM((2, S, LANES), BF16),
                            pltpu.VMEM((S, 2 * LANES), BF16), pltpu.VMEM((2, 2, TQ, TK), F32)]),
        out_shape=jax.ShapeDtypeStruct((B, S, GROUP_W), BF16),
        compiler_params=pltpu.CompilerParams(
            dimension_semantics=("parallel", "parallel"), vmem_limit_bytes=VMEM_LIMIT),
        name="moba_attn" if moba else "fox_attn",
    )(ti, tj, q, k, v, sg, shift, *extra)


def _l1_kernel(x_ref, yf_ref, ym_ref, wo0_ref, g_ref, w1_ref, cw_ref, cb_ref, lng_ref, lnb_ref,
               wo1_ref, o_ref, ubuf_ref, cbuf_ref):
    tm = x_ref.shape[1]
    C = x_ref.shape[2]
    i = pl.program_id(1)

    @pl.when(i == 0)
    def _():
        ubuf_ref[0:HALO, :] = jnp.zeros((HALO, C), F32)

    base = HALO - (CONV_WIDTH - 1)

    def conv_rows(r0, cs):
        acc = None
        for phase in range(SUBLANES):
            rows = CONV_ROWS + (SUBLANES if phase else 0)
            part = None
            for j in range(CONV_WIDTH):
                if (base + j) % SUBLANES != phase:
                    continue
                term = cw_ref[j:j + 1, cs] * ubuf_ref[pl.ds(r0 + base + j - phase, rows), cs]
                part = term if part is None else part + term
            if phase:
                part = pltpu.roll(part, rows - phase, axis=0)[0:CONV_ROWS]
            acc = part if acc is None else acc + part
        return acc

    nblk = tm // L1_ROWS
    x1s, zs = [], []
    for blk in range(nblk):
        rs = slice(blk * L1_ROWS, (blk + 1) * L1_ROWS)
        y = jnp.concatenate([yf_ref[0, rs, :], ym_ref[0, rs, :]], axis=1)
        x1 = x_ref[0, rs, :] + _dot(y, wo0_ref[...])

        ms = jnp.mean(x1 * x1, axis=-1, keepdims=True)
        h = (x1 * lax.rsqrt(ms + NORM_EPS) * g_ref[...]).astype(BF16)

        val = _dot(h, w1_ref[:, 0:C])
        glu = _dot(h, w1_ref[:, C:2 * C])
        ubuf_ref[HALO + blk * L1_ROWS:HALO + (blk + 1) * L1_ROWS, :] = val * jax.nn.sigmoid(glu)
        x1s.append(x1)
        zs.append(_dot(h, w1_ref[:, 2 * C:3 * C]))

    for blk in range(nblk):
        rs = slice(blk * L1_ROWS, (blk + 1) * L1_ROWS)
        x1, z = x1s[blk], zs[blk]
        for c in range(C // LANES):
            cs = slice(c * LANES, (c + 1) * LANES)
            for r in range(L1_ROWS // CONV_ROWS):
                r0 = blk * L1_ROWS + r * CONV_ROWS
                cbuf_ref[r0:r0 + CONV_ROWS, cs] = conv_rows(r0, cs)

        cv = cbuf_ref[rs, :] + cb_ref[...]
        mu = jnp.mean(cv, axis=-1, keepdims=True)
        xc = cv - mu
        var = jnp.mean(xc * xc, axis=-1, keepdims=True)
        yln = xc * lax.rsqrt(var + LN_EPS) * lng_ref[...] + lnb_ref[...]
        a = (yln * jax.nn.sigmoid(yln)) * (z * jax.nn.sigmoid(z))
        o_ref[0, rs, :] = x1 + _dot(a.astype(BF16), wo1_ref[...])

    ubuf_ref[0:HALO, :] = ubuf_ref[tm:tm + HALO, :]


def _l1_conv(x, yf, ym, wo0, g, w1, cw, cb, lng, lnb, wo1):
    B, S, D = x.shape
    tm = TM_L1
    row = lambda w: pl.BlockSpec((1, tm, w), lambda b, i: (b, i, 0))
    const = lambda a: pl.BlockSpec(a.shape, lambda b, i: (0,) * a.ndim)
    return pl.pallas_call(
        _l1_kernel,
        grid=(B, S // tm),
        in_specs=[row(D), row(GROUP_W), row(GROUP_W), const(wo0), const(g), const(w1), const(cw),
                  const(cb), const(lng), const(lnb), const(wo1)],
        out_specs=row(D),
        out_shape=jax.ShapeDtypeStruct((B, S, D), F32),
        scratch_shapes=[pltpu.VMEM((tm + HALO, D), F32), pltpu.VMEM((tm, D), F32)],
        compiler_params=pltpu.CompilerParams(
            dimension_semantics=("parallel", "arbitrary"), vmem_limit_bytes=VMEM_LIMIT),
        name="l1_conv",
    )(x, yf, ym, wo0, g, w1, cw, cb, lng, lnb, wo1)


def _rope_tables(S):
    half = ROT_DIM // 2
    inv_freq = ROPE_THETA ** (-jnp.arange(half, dtype=F32) / half)
    ang = jnp.arange(S).astype(F32)[:, None] * inv_freq[None, :]
    cos, sin = jnp.cos(ang), jnp.sin(ang)
    ones = jnp.ones((S, HEAD_DIM - ROT_DIM), F32)
    cos_h = jnp.concatenate([cos, cos, ones], axis=1)
    sin_h = jnp.concatenate([-sin, sin, 0.0 * ones], axis=1)
    reps = LANES // HEAD_DIM
    return jnp.tile(cos_h, (1, reps)), jnp.tile(sin_h, (1, reps))


def _forget_spread():
    selq = jnp.zeros((LANES, LANES), F32)
    selk = jnp.zeros((LANES, LANES), F32)
    for hd in range(N_HEADS):
        for piece in range(3):
            selq = selq.at[piece * N_HEADS + hd, 2 * N_HEADS * hd + piece].set(1.0)
            selq = selq.at[3 * N_HEADS, 2 * N_HEADS * hd + 3 + piece].set(1.0)
            selk = selk.at[3 * N_HEADS, 2 * N_HEADS * hd + piece].set(1.0)
            selk = selk.at[piece * N_HEADS + hd, 2 * N_HEADS * hd + 3 + piece].set(-1.0)
    return selq.astype(BF16), selk.astype(BF16)


def kernel(x, l0_norm, l0_w_in, l0_b_f, l0_qn_fox, l0_kn_fox, l0_qn_moba, l0_kn_moba, l0_w_out,
           l1_norm, l1_w_in, l1_conv_w, l1_conv_b, l1_ln_g, l1_ln_b, l1_w_out):
    B, S, D = x.shape
    assert D == D_MODEL and S % MOBA_BLOCK == 0 and S // MOBA_BLOCK <= N_HEADS * 2

    fq = 3 * GROUP_W
    w = l0_w_in
    wcat = jnp.concatenate(
        [w[:, :fq], w[:, fq + N_HEADS:2 * fq + N_HEADS], w[:, 2 * fq + N_HEADS:],
         jnp.pad(w[:, fq:fq + N_HEADS], ((0, 0), (0, LANES - N_HEADS)))], axis=1).astype(BF16)
    bfp = jnp.pad(l0_b_f, (0, LANES - N_HEADS)).reshape(1, LANES)
    tile_gain = lambda gain: jnp.tile(gain, N_HEADS).reshape(1, GROUP_W)
    cos, sin = _rope_tables(S)
    pmat = jnp.kron(jnp.eye(256 // HEAD_DIM, dtype=F32), jnp.ones((HEAD_DIM, HEAD_DIM), F32)).astype(BF16)
    ltri = jnp.tril(jnp.ones((TM_IN, TM_IN), F32)).astype(BF16)
    selq, selk = _forget_spread()

    qf, kf, vf, qx, kx, qm, km, vm, kmean, sg = _l0_in(
        x, l0_norm.reshape(1, D), wcat, bfp, tile_gain(l0_qn_fox), tile_gain(l0_kn_fox),
        tile_gain(l0_qn_moba), tile_gain(l0_kn_moba), cos, sin, pmat, ltri, selq, selk)

    def attend(q, k, v, gq, gk, extra, moba):
        bound = jnp.ceil(1.02 * (HEAD_DIM ** 0.5) * jnp.max(jnp.abs(gq)) * jnp.max(jnp.abs(gk)))
        shift = jnp.zeros((1, LANES), F32).at[0, SHIFT_LANE].set(-bound)
        run = lambda online: functools.partial(_attention, q, k, v, sg, shift, extra, moba=moba, online=online)
        return lax.cond(bound <= MAX_FIXED_SHIFT, run(False), run(True))

    yf = attend(qf, kf, vf, l0_qn_fox, l0_kn_fox, (qx, kx), False)
    ym = attend(qm, km, vm, l0_qn_moba, l0_kn_moba, (kmean.reshape(B, S // MOBA_BLOCK, GROUP_W),), True)

    cw = jnp.pad(l1_conv_w.reshape(CONV_WIDTH, D), ((0, HALO - CONV_WIDTH), (0, 0)))
    return _l1_conv(
        x, yf, ym, l0_w_out.astype(BF16), l1_norm.reshape(1, D), l1_w_in.astype(BF16), cw,
        l1_conv_b.reshape(1, D), l1_ln_g.reshape(1, D), l1_ln_b.reshape(1, D), l1_w_out.astype(BF16))
```

```python
import functools

import jax
import jax.numpy as jnp
import numpy as np
from jax import lax
from jax.experimental import pallas as pl
from jax.experimental.pallas import tpu as pltpu

F32 = jnp.float32
BF16 = jnp.bfloat16

D_MODEL = 1024
HEAD_DIM = 64
N_HEADS = 8
GROUP_W = N_HEADS * HEAD_DIM
MOBA_BLOCK = 256
MOBA_TOPK = 3
ROPE_THETA = 500000.0
ROT_DIM = HEAD_DIM // 4
CONV_WIDTH = 31
NORM_EPS = 1e-6
LN_EPS = 1e-5

LANES = 128
SUBLANES = 8
HALO = 32
MASK_BIAS = -(2.0 ** 100)
SHIFT_LANE = LANES - 1
MAX_FIXED_SHIFT = 40.0
NEG_BIG = -1e30
TINY = 1e-37

TM_IN = 512
TQ = 512
TK = 512
SELECT_UNROLL = 4
STEPS_PER_TRIP = 12
TM_L1 = 512
L1_ROWS = 256
CONV_ROWS = 128

VMEM_LIMIT = 56 * 1024 * 1024


def _dot(a, b):
    return jnp.dot(a, b, preferred_element_type=F32)


def _dot_nt(a, b):
    return lax.dot_general(a, b, (((1,), (1,)), ((), ())), preferred_element_type=F32)


def _split2(v):
    hi = v.astype(BF16)
    lo = (v - hi.astype(F32)).astype(BF16)
    return hi, lo


def _split3(v):
    p1 = v.astype(BF16)
    r1 = v - p1.astype(F32)
    p2 = r1.astype(BF16)
    r2 = r1 - p2.astype(F32)
    p3 = r2.astype(BF16)
    return p1, p2, p3


def _l0_in_kernel(x_ref, g_ref, w_ref, bf_ref, gfq_ref, gfk_ref, gmq_ref, gmk_ref,
                  cos_ref, sin_ref, p_ref, selq_ref, selk_ref,
                  qf_ref, kf_ref, vf_ref, qx_ref, kx_ref, qm_ref, km_ref, vm_ref,
                  kmean_ref, sg_ref, carry_ref):
    tm = x_ref.shape[1]

    @pl.when(pl.program_id(1) == 0)
    def _():
        carry_ref[...] = jnp.zeros_like(carry_ref)

    x = x_ref[0]
    ms = jnp.mean(x * x, axis=-1, keepdims=True)
    h = (x * lax.rsqrt(ms + NORM_EPS) * g_ref[...]).astype(BF16)

    pmat = p_ref[...]

    def head_norm(y, gain):
        outs = []
        for c in range(GROUP_W // 256):
            yc = y[:, c * 256:(c + 1) * 256]
            hi, lo = _split2(yc * yc)
            ssum = _dot(hi, pmat) + _dot(lo, pmat)
            r = lax.rsqrt(ssum * (1.0 / HEAD_DIM) + NORM_EPS)
            outs.append(yc * r * gain[:, c * 256:(c + 1) * 256])
        return jnp.concatenate(outs, axis=1)

    def rope(y):
        cos = cos_ref[...]
        sin = sin_ref[...]
        lane = lax.broadcasted_iota(jnp.int32, (tm, LANES), 1) % HEAD_DIM
        outs = []
        for c in range(GROUP_W // LANES):
            yc = y[:, c * LANES:(c + 1) * LANES]
            sw = jnp.where(lane < ROT_DIM // 2,
                           pltpu.roll(yc, LANES - ROT_DIM // 2, axis=1),
                           pltpu.roll(yc, ROT_DIM // 2, axis=1))
            outs.append(yc * cos + sw * sin)
        return jnp.concatenate(outs, axis=1)

    def proj(col):
        return _dot(h, w_ref[:, col * GROUP_W:(col + 1) * GROUP_W])

    lane = lax.broadcasted_iota(jnp.int32, (tm, LANES), 1)
    yf = _dot(h, w_ref[:, 8 * GROUP_W:8 * GROUP_W + LANES]) + bf_ref[...]
    yq = proj(0)
    lf = jnp.where(lane < N_HEADS, jax.nn.log_sigmoid(yf), 0.0)
    yk = proj(1)
    row = lax.broadcasted_iota(jnp.int32, (tm, LANES), 0)
    c = lf
    span = 1
    while span < tm:
        c = c + jnp.where(row >= span, pltpu.roll(c, span, axis=0), 0.0)
        span *= 2
    c = c + carry_ref[...]
    carry_ref[...] = c[tm - 1:tm, :]
    qf_ref[0] = (head_norm(yq, gfq_ref[...]) * (HEAD_DIM ** -0.5)).astype(BF16)
    h1, h2, h3 = _split3(c)
    c3 = (h1.astype(F32) + pltpu.roll(h2.astype(F32), N_HEADS, axis=1)
          + pltpu.roll(h3.astype(F32), 2 * N_HEADS, axis=1)
          + jnp.where(lane == 3 * N_HEADS, 1.0, 0.0)).astype(BF16)
    kf_ref[0] = head_norm(yk, gfk_ref[...]).astype(BF16)
    vf_ref[0] = proj(2).astype(BF16)
    qx_ref[0] = _dot(c3, selq_ref[...]).astype(BF16)
    kx_ref[0] = _dot(c3, selk_ref[...]).astype(BF16)

    qm_ref[0] = (rope(head_norm(proj(3), gmq_ref[...])) * (HEAD_DIM ** -0.5)).astype(BF16)
    km = rope(head_norm(proj(4), gmk_ref[...]))
    km_ref[0] = km.astype(BF16)
    for blk in range(tm // MOBA_BLOCK):
        kmean_ref[0, blk] = jnp.sum(km[blk * MOBA_BLOCK:(blk + 1) * MOBA_BLOCK], axis=0,
                                    keepdims=True) * (1.0 / MOBA_BLOCK)
    vm_ref[0] = proj(5).astype(BF16)

    for c in range(2):
        yg = proj(6 + c)
        sg_ref[0, :, c * GROUP_W:(c + 1) * GROUP_W] = yg * jax.nn.sigmoid(yg)


def _l0_in(x, g, wcat, bfp, gfq, gfk, gmq, gmk, cos, sin, pmat, selq, selk):
    B, S, D = x.shape
    tm = TM_IN
    nblk = S // MOBA_BLOCK
    grid = (B, S // tm)
    row = lambda w: pl.BlockSpec((1, tm, w), lambda b, i: (b, i, 0))
    const = lambda a: pl.BlockSpec(a.shape, lambda b, i: (0,) * a.ndim)
    bf_act = lambda w: jax.ShapeDtypeStruct((B, S, w), BF16)
    out_shape = (
        bf_act(GROUP_W), bf_act(GROUP_W), bf_act(GROUP_W),
        bf_act(LANES), bf_act(LANES),
        bf_act(GROUP_W), bf_act(GROUP_W), bf_act(GROUP_W),
        jax.ShapeDtypeStruct((B, nblk, 1, GROUP_W), F32),
        jax.ShapeDtypeStruct((B, S, 2 * GROUP_W), F32),
    )
    out_specs = (
        row(GROUP_W), row(GROUP_W), row(GROUP_W), row(LANES), row(LANES),
        row(GROUP_W), row(GROUP_W), row(GROUP_W),
        pl.BlockSpec((1, tm // MOBA_BLOCK, 1, GROUP_W), lambda b, i: (b, i, 0, 0)),
        row(2 * GROUP_W),
    )
    in_specs = [
        row(D), const(g), const(wcat), const(bfp), const(gfq), const(gfk), const(gmq), const(gmk),
        pl.BlockSpec((tm, LANES), lambda b, i: (i, 0)), pl.BlockSpec((tm, LANES), lambda b, i: (i, 0)),
        const(pmat), const(selq), const(selk),
    ]
    return pl.pallas_call(
        _l0_in_kernel,
        grid=grid, in_specs=in_specs, out_specs=out_specs, out_shape=out_shape,
        scratch_shapes=[pltpu.VMEM((1, LANES), F32)],
        compiler_params=pltpu.CompilerParams(
            dimension_semantics=("parallel", "arbitrary"), vmem_limit_bytes=VMEM_LIMIT),
        name="l0_in",
    )(x, g, wcat, bfp, gfq, gfk, gmq, gmk, cos, sin, pmat, selq, selk)


def _attn_kernel(ti_ref, tj_ref, *refs, moba, online):
    if moba:
        q_ref, k_ref, v_ref, sg_ref, shift_ref, kmean_ref, y_ref, k2_ref, v2_ref, q2_ref, s_ref = refs
    else:
        q_ref, k_ref, v_ref, sg_ref, shift_ref, qx_ref, kx_ref, y_ref, k2_ref, v2_ref, q2_ref, s_ref = refs
    S = k_ref.shape[1]
    nq = S // TQ
    nsteps = nq * (nq + 1) // 2
    assert STEPS_PER_TRIP % 2 == 0 and nsteps % STEPS_PER_TRIP == 0
    pair = pl.program_id(1)
    half = LANES // 2
    ext_w = 2 * N_HEADS
    rows = 512

    zero = jnp.zeros((rows, LANES), BF16)
    one = jnp.ones((rows, LANES), BF16)
    lane = lax.broadcasted_iota(jnp.int32, (rows, LANES), 1)
    lane_b = lane.astype(BF16)
    group_b = (lane // ext_w).astype(BF16)
    for r in range(S // rows):
        sl = slice(r * rows, (r + 1) * rows)
        k = k_ref[0, sl, :]
        v = v_ref[0, sl, :]
        if moba:
            blk_b = ((lax.broadcasted_iota(jnp.int32, (rows, LANES), 0) + r * rows) // MOBA_BLOCK).astype(BF16)
        else:
            kx = kx_ref[0, sl, :]
        for hh in range(2):
            mine = (lane_b >= half) if hh else (lane_b < half)
            k2_ref[hh, sl, 0:LANES] = jnp.where(mine, k, zero)
            if moba:
                ext = jnp.where(lane_b == blk_b + hh * ext_w, one, zero)
            else:
                head_b = jnp.full((rows, LANES), 2 * pair + hh, jnp.int32).astype(BF16)
                ext = jnp.where(group_b == head_b, kx, zero)
            if not online:
                ext = jnp.where(lane_b == SHIFT_LANE, one, ext)
            k2_ref[hh, sl, LANES:2 * LANES] = ext
            v2_ref[hh, sl, :] = jnp.where(mine, v, jnp.where(lane_b == (0 if hh else half), one, zero))

    def q_rows(i):
        return pl.ds(pl.multiple_of(i * TQ, TQ), TQ)

    def kv_rows(j):
        return pl.ds(pl.multiple_of(j * TK, TK), TK)

    if moba:
        km = kmean_ref[0].astype(F32)
        nblk = km.shape[0]
        lane_k = lax.broadcasted_iota(jnp.int32, (nblk, LANES), 1)
        kmt = jnp.concatenate(
            [jnp.where(lane_k < half, km, 0.0), jnp.where(lane_k >= half, km, 0.0),
             jnp.zeros((LANES - 2 * nblk, LANES), F32)], axis=0).astype(BF16)

        def select(i):
            q = q_ref[0, q_rows(i), :]
            st = _dot_nt(kmt, q)[0:2 * nblk]
            n = lax.broadcasted_iota(jnp.int32, (2 * nblk, TQ), 0) % nblk
            own = i * (TQ // MOBA_BLOCK) + lax.broadcasted_iota(jnp.int32, (2 * nblk, TQ), 1) // MOBA_BLOCK
            valid = n < own
            gsc = jnp.where(valid, st, -jnp.inf)
            rank = jnp.zeros((2 * nblk, TQ), jnp.int32)
            for m in range(nblk):
                gm = jnp.concatenate(
                    [jnp.broadcast_to(gsc[m:m + 1], (nblk, TQ)),
                     jnp.broadcast_to(gsc[nblk + m:nblk + m + 1], (nblk, TQ))], axis=0)
                beats = (gm > gsc) | ((gm == gsc) & (m < n))
                rank = rank + beats.astype(jnp.int32)
            sel = (valid & (rank < MOBA_TOPK)) | (n == own)
            sbt = jnp.concatenate([jnp.where(sel, 0.0, MASK_BIAS),
                                   jnp.zeros((LANES - 2 * nblk, TQ), F32)], axis=0)
            qext = sbt.T
            if not online:
                qext = qext + shift_ref[...]
            q2_ref[q_rows(i), 0:LANES] = q
            q2_ref[q_rows(i), LANES:2 * LANES] = qext.astype(BF16)

        def select_some(ii, _):
            for u in range(SELECT_UNROLL):
                select(SELECT_UNROLL * ii + u)
            return 0

        assert nq % SELECT_UNROLL == 0
        lax.fori_loop(0, nq // SELECT_UNROLL, select_some, 0)
    else:
        shift_b = jnp.broadcast_to(shift_ref[...], (rows, LANES)).astype(BF16)
        for r in range(S // rows):
            sl = slice(r * rows, (r + 1) * rows)
            qext = qx_ref[0, sl, :]
            if not online:
                qext = qext + shift_b
            q2_ref[sl, 0:LANES] = q_ref[0, sl, :]
            q2_ref[sl, LANES:2 * LANES] = qext

    col_minus_row = (lax.broadcasted_iota(jnp.int32, (TQ, TK), 1)
                     - lax.broadcasted_iota(jnp.int32, (TQ, TK), 0))
    lane = lax.broadcasted_iota(jnp.int32, (TQ, LANES), 1)

    def logits(t, slot):
        q2 = q2_ref[q_rows(ti_ref[t + 1]), :]
        ks = kv_rows(tj_ref[t + 1])
        for hh in range(2):
            s_ref[slot, hh] = _dot_nt(q2, k2_ref[hh, ks, :])

    def init_state():
        if online:
            return tuple((jnp.full((TQ, 1), -jnp.inf, F32), jnp.zeros((TQ, LANES), F32)) for _ in range(2))
        return tuple(jnp.zeros((TQ, LANES), F32) for _ in range(2))

    def write_out(state, i):
        accs = [st[1] for st in state] if online else state
        l0 = jnp.maximum(accs[0][:, half:half + 1], TINY)
        l1 = jnp.maximum(accs[1][:, 0:1], TINY)
        o = jnp.where(lane < half, accs[0] / l0, accs[1] / l1)
        y_ref[0, q_rows(i), :] = (o * sg_ref[0, q_rows(i), :]).astype(BF16)

    def step(t, slot, state):
        i_prev = ti_ref[t]
        write_out(state, i_prev)
        closed = i_prev == tj_ref[t]
        state = jax.tree.map(lambda a, b: jnp.where(closed, a, b), init_state(), state)

        i = ti_ref[t + 1]
        j = tj_ref[t + 1]
        keep = col_minus_row <= (i - j) * TK
        ks = kv_rows(j)
        new = []
        for hh in range(2):
            s = jnp.where(keep, s_ref[slot, hh], NEG_BIG)
            if online:
                m, acc = state[hh]
                m_new = jnp.maximum(m, jnp.max(s, axis=1, keepdims=True))
                alpha = jnp.exp(m - m_new)
                p = jnp.exp(s - m_new).astype(BF16)
                new.append((m_new, alpha * acc + _dot(p, v2_ref[hh, ks, :])))
            else:
                new.append(state[hh] + _dot(jnp.exp(s).astype(BF16), v2_ref[hh, ks, :]))
        return tuple(new)

    def trip(tt, state):
        for u in range(STEPS_PER_TRIP):
            t = STEPS_PER_TRIP * tt + u
            logits(t + 1, (u + 1) % 2)
            state = step(t, u % 2, state)
        return state

    logits(0, 0)
    state = lax.fori_loop(0, nsteps // STEPS_PER_TRIP, trip, init_state())
    write_out(state, ti_ref[nsteps])


def _attention(q, k, v, sg, shift, extra, *, moba, online):
    B, S, _ = q.shape
    npair = GROUP_W // LANES
    nq = S // TQ
    order = [(0, 1)] + [(i, j) for i in range(nq) for j in range(i + 1)] + [(0, 0), (0, 0)]
    ti = jnp.asarray([i for i, _ in order], jnp.int32)
    tj = jnp.asarray([j for _, j in order], jnp.int32)
    sg_off = npair if moba else 0
    seq = lambda off: pl.BlockSpec((1, S, LANES), lambda b, p, ti, tj: (b, 0, p + off))
    if moba:
        (kmean,) = extra
        extra_specs = [pl.BlockSpec((1, kmean.shape[1], LANES), lambda b, p, ti, tj: (b, 0, p))]
    else:
        all_heads = pl.BlockSpec((1, S, LANES), lambda b, p, ti, tj: (b, 0, 0))
        extra_specs = [all_heads, all_heads]
    return pl.pallas_call(
        functools.partial(_attn_kernel, moba=moba, online=online),
        grid_spec=pltpu.PrefetchScalarGridSpec(
            num_scalar_prefetch=2,
            grid=(B, npair),
            in_specs=[seq(0), seq(0), seq(0), seq(sg_off),
                      pl.BlockSpec((1, LANES), lambda b, p, ti, tj: (0, 0))] + extra_specs,
            out_specs=seq(0),
            scratch_shapes=[pltpu.VMEM((2, S, 2 * LANES), BF16), pltpu.VMEM((2, S, LANES), BF16),
                            pltpu.VMEM((S, 2 * LANES), BF16), pltpu.VMEM((2, 2, TQ, TK), F32)]),
        out_shape=jax.ShapeDtypeStruct((B, S, GROUP_W), BF16),
        compiler_params=pltpu.CompilerParams(
            dimension_semantics=("parallel", "parallel"), vmem_limit_bytes=VMEM_LIMIT),
        name="moba_attn" if moba else "fox_attn",
    )(ti, tj, q, k, v, sg, shift, *extra)


def _l1_kernel(x_ref, yf_ref, ym_ref, wo0_ref, g_ref, w1_ref, cw_ref, cb_ref, lng_ref, lnb_ref,
               wo1_ref, o_ref, ubuf_ref, cbuf_ref):
    tm = x_ref.shape[1]
    C = x_ref.shape[2]
    i = pl.program_id(1)

    @pl.when(i == 0)
    def _():
        ubuf_ref[0:HALO, :] = jnp.zeros((HALO, C), F32)

    base = HALO - (CONV_WIDTH - 1)

    def conv_rows(r0, cs):
        acc = None
        for phase in range(SUBLANES):
            rows = CONV_ROWS + (SUBLANES if phase else 0)
            part = None
            for j in range(CONV_WIDTH):
                if (base + j) % SUBLANES != phase:
                    continue
                term = cw_ref[j:j + 1, cs] * ubuf_ref[pl.ds(r0 + base + j - phase, rows), cs]
                part = term if part is None else part + term
            if phase:
                part = pltpu.roll(part, rows - phase, axis=0)[0:CONV_ROWS]
            acc = part if acc is None else acc + part
        return acc

    nblk = tm // L1_ROWS
    x1s, zs = [], []
    for blk in range(nblk):
        rs = slice(blk * L1_ROWS, (blk + 1) * L1_ROWS)
        y = jnp.concatenate([yf_ref[0, rs, :], ym_ref[0, rs, :]], axis=1)
        x1 = x_ref[0, rs, :] + _dot(y, wo0_ref[...])

        ms = jnp.mean(x1 * x1, axis=-1, keepdims=True)
        h = (x1 * lax.rsqrt(ms + NORM_EPS) * g_ref[...]).astype(BF16)

        val = _dot(h, w1_ref[:, 0:C])
        glu = _dot(h, w1_ref[:, C:2 * C])
        ubuf_ref[HALO + blk * L1_ROWS:HALO + (blk + 1) * L1_ROWS, :] = val * jax.nn.sigmoid(glu)
        x1s.append(x1)
        zs.append(_dot(h, w1_ref[:, 2 * C:3 * C]))

    for blk in range(nblk):
        rs = slice(blk * L1_ROWS, (blk + 1) * L1_ROWS)
        x1, z = x1s[blk], zs[blk]
        for c in range(C // LANES):
            cs = slice(c * LANES, (c + 1) * LANES)
            for r in range(L1_ROWS // CONV_ROWS):
                r0 = blk * L1_ROWS + r * CONV_ROWS
                cbuf_ref[r0:r0 + CONV_ROWS, cs] = conv_rows(r0, cs)

        cv = cbuf_ref[rs, :] + cb_ref[...]
        mu = jnp.mean(cv, axis=-1, keepdims=True)
        xc = cv - mu
        var = jnp.mean(xc * xc, axis=-1, keepdims=True)
        yln = xc * lax.rsqrt(var + LN_EPS) * lng_ref[...] + lnb_ref[...]
        a = (yln * jax.nn.sigmoid(yln)) * (z * jax.nn.sigmoid(z))
        o_ref[0, rs, :] = x1 + _dot(a.astype(BF16), wo1_ref[...])

    ubuf_ref[0:HALO, :] = ubuf_ref[tm:tm + HALO, :]


def _l1_conv(x, yf, ym, wo0, g, w1, cw, cb, lng, lnb, wo1):
    B, S, D = x.shape
    tm = TM_L1
    row = lambda w: pl.BlockSpec((1, tm, w), lambda b, i: (b, i, 0))
    const = lambda a: pl.BlockSpec(a.shape, lambda b, i: (0,) * a.ndim)
    return pl.pallas_call(
        _l1_kernel,
        grid=(B, S // tm),
        in_specs=[row(D), row(GROUP_W), row(GROUP_W), const(wo0), const(g), const(w1), const(cw),
                  const(cb), const(lng), const(lnb), const(wo1)],
        out_specs=row(D),
        out_shape=jax.ShapeDtypeStruct((B, S, D), F32),
        scratch_shapes=[pltpu.VMEM((tm + HALO, D), F32), pltpu.VMEM((tm, D), F32)],
        compiler_params=pltpu.CompilerParams(
            dimension_semantics=("parallel", "arbitrary"), vmem_limit_bytes=VMEM_LIMIT),
        name="l1_conv",
    )(x, yf, ym, wo0, g, w1, cw, cb, lng, lnb, wo1)


def _rope_tables(S):
    half = ROT_DIM // 2
    inv_freq = ROPE_THETA ** (-jnp.arange(half, dtype=F32) / half)
    ang = jnp.arange(S).astype(F32)[:, None] * inv_freq[None, :]
    cos, sin = jnp.cos(ang), jnp.sin(ang)
    ones = jnp.ones((S, HEAD_DIM - ROT_DIM), F32)
    cos_h = jnp.concatenate([cos, cos, ones], axis=1)
    sin_h = jnp.concatenate([-sin, sin, 0.0 * ones], axis=1)
    reps = LANES // HEAD_DIM
    return jnp.tile(cos_h, (1, reps)), jnp.tile(sin_h, (1, reps))


def _forget_spread():
    selq = np.zeros((LANES, LANES), np.float32)
    selk = np.zeros((LANES, LANES), np.float32)
    for hd in range(N_HEADS):
        for piece in range(3):
            selq[piece * N_HEADS + hd, 2 * N_HEADS * hd + piece] = 1.0
            selq[3 * N_HEADS, 2 * N_HEADS * hd + 3 + piece] = 1.0
            selk[3 * N_HEADS, 2 * N_HEADS * hd + piece] = 1.0
            selk[piece * N_HEADS + hd, 2 * N_HEADS * hd + 3 + piece] = -1.0
    return jnp.asarray(selq, BF16), jnp.asarray(selk, BF16)


def kernel(x, l0_norm, l0_w_in, l0_b_f, l0_qn_fox, l0_kn_fox, l0_qn_moba, l0_kn_moba, l0_w_out,
           l1_norm, l1_w_in, l1_conv_w, l1_conv_b, l1_ln_g, l1_ln_b, l1_w_out):
    B, S, D = x.shape
    assert D == D_MODEL and S % MOBA_BLOCK == 0 and S // MOBA_BLOCK <= N_HEADS * 2

    fq = 3 * GROUP_W
    w = l0_w_in.astype(BF16)
    wcat = jnp.concatenate(
        [w[:, :fq], w[:, fq + N_HEADS:],
         jnp.pad(w[:, fq:fq + N_HEADS], ((0, 0), (0, LANES - N_HEADS)))], axis=1)
    bfp = jnp.pad(l0_b_f, (0, LANES - N_HEADS)).reshape(1, LANES)
    tile_gain = lambda gain: jnp.tile(gain, N_HEADS).reshape(1, GROUP_W)
    cos, sin = _rope_tables(S)
    pmat = jnp.asarray(np.kron(np.eye(256 // HEAD_DIM), np.ones((HEAD_DIM, HEAD_DIM))), BF16)
    selq, selk = _forget_spread()

    qf, kf, vf, qx, kx, qm, km, vm, kmean, sg = _l0_in(
        x, l0_norm.reshape(1, D), wcat, bfp, tile_gain(l0_qn_fox), tile_gain(l0_kn_fox),
        tile_gain(l0_qn_moba), tile_gain(l0_kn_moba), cos, sin, pmat, selq, selk)

    def attend(q, k, v, gq, gk, extra, moba):
        bound = jnp.ceil(1.02 * (HEAD_DIM ** 0.5) * jnp.max(jnp.abs(gq)) * jnp.max(jnp.abs(gk)))
        shift = jnp.zeros((1, LANES), F32).at[0, SHIFT_LANE].set(-bound)
        run = lambda online: functools.partial(_attention, q, k, v, sg, shift, extra, moba=moba, online=online)
        return lax.cond(bound <= MAX_FIXED_SHIFT, run(False), run(True))

    yf = attend(qf, kf, vf, l0_qn_fox, l0_kn_fox, (qx, kx), False)
    ym = attend(qm, km, vm, l0_qn_moba, l0_kn_moba, (kmean.reshape(B, S // MOBA_BLOCK, GROUP_W),), True)

    cw = jnp.pad(l1_conv_w.reshape(CONV_WIDTH, D), ((0, HALO - CONV_WIDTH), (0, 0)))
    return _l1_conv(
        x, yf, ym, l0_w_out.astype(BF16), l1_norm.reshape(1, D), l1_w_in.astype(BF16), cw,
        l1_conv_b.reshape(1, D), l1_ln_g.reshape(1, D), l1_ln_b.reshape(1, D), l1_w_out.astype(BF16))
```

```python
import functools

import jax
import jax.numpy as jnp
import numpy as np
from jax import lax
from jax.experimental import pallas as pl
from jax.experimental.pallas import tpu as pltpu

F32 = jnp.float32
BF16 = jnp.bfloat16

D_MODEL = 1024
HEAD_DIM = 64
N_HEADS = 8
GROUP_W = N_HEADS * HEAD_DIM
MOBA_BLOCK = 256
MOBA_TOPK = 3
ROPE_THETA = 500000.0
ROT_DIM = HEAD_DIM // 4
CONV_WIDTH = 31
NORM_EPS = 1e-6
LN_EPS = 1e-5

LANES = 128
SUBLANES = 8
HALO = 32
MASK_BIAS = -(2.0 ** 100)
SHIFT_LANE = LANES - 1
MAX_FIXED_SHIFT = 40.0
NEG_BIG = -1e30
TINY = 1e-37

TM_IN = 512
TQ = 512
TK = 512
SELECT_UNROLL = 8
STEPS_PER_TRIP = 12
TM_L1 = 512
L1_ROWS = 256
CONV_ROWS = 128

VMEM_LIMIT = 56 * 1024 * 1024


def _dot(a, b):
    return jnp.dot(a, b, preferred_element_type=F32)


def _dot_nt(a, b):
    return lax.dot_general(a, b, (((1,), (1,)), ((), ())), preferred_element_type=F32)


def _split2(v):
    hi = v.astype(BF16)
    lo = (v - hi.astype(F32)).astype(BF16)
    return hi, lo


def _split3(v):
    p1 = v.astype(BF16)
    r1 = v - p1.astype(F32)
    p2 = r1.astype(BF16)
    r2 = r1 - p2.astype(F32)
    p3 = r2.astype(BF16)
    return p1, p2, p3


def _l0_in_kernel(x_ref, g_ref, wfox_ref, wrest_ref, wf_ref, bf_ref, gfq_ref, gfk_ref, gmq_ref, gmk_ref,
                  cos_ref, sin_ref, p_ref, selq_ref, selk_ref,
                  qf_ref, kf_ref, vf_ref, qx_ref, kx_ref, qm_ref, km_ref, vm_ref,
                  kmean_ref, sg_ref, carry_ref):
    tm = x_ref.shape[1]

    @pl.when(pl.program_id(1) == 0)
    def _():
        carry_ref[...] = jnp.zeros_like(carry_ref)

    x = x_ref[0]
    ms = jnp.mean(x * x, axis=-1, keepdims=True)
    h = (x * lax.rsqrt(ms + NORM_EPS) * g_ref[...]).astype(BF16)

    pmat = p_ref[...]

    def head_norm(y, gain):
        outs = []
        for c in range(GROUP_W // 256):
            yc = y[:, c * 256:(c + 1) * 256]
            hi, lo = _split2(yc * yc)
            ssum = _dot(hi, pmat) + _dot(lo, pmat)
            r = lax.rsqrt(ssum * (1.0 / HEAD_DIM) + NORM_EPS)
            outs.append(yc * r * gain[:, c * 256:(c + 1) * 256])
        return jnp.concatenate(outs, axis=1)

    def rope(y):
        cos = cos_ref[...]
        sin = sin_ref[...]
        lane = lax.broadcasted_iota(jnp.int32, (tm, LANES), 1) % HEAD_DIM
        outs = []
        for c in range(GROUP_W // LANES):
            yc = y[:, c * LANES:(c + 1) * LANES]
            sw = jnp.where(lane < ROT_DIM // 2,
                           pltpu.roll(yc, LANES - ROT_DIM // 2, axis=1),
                           pltpu.roll(yc, ROT_DIM // 2, axis=1))
            outs.append(yc * cos + sw * sin)
        return jnp.concatenate(outs, axis=1)

    def proj(col):
        ref, c0 = (wfox_ref, col) if col < 3 else (wrest_ref, col - 3)
        return _dot(h, ref[:, c0 * GROUP_W:(c0 + 1) * GROUP_W])

    lane = lax.broadcasted_iota(jnp.int32, (tm, LANES), 1)
    yf = _dot(h, wf_ref[...]) + bf_ref[...]
    yq = proj(0)
    lf = jnp.where(lane < N_HEADS, jax.nn.log_sigmoid(yf), 0.0)
    yk = proj(1)
    row = lax.broadcasted_iota(jnp.int32, (tm, LANES), 0)
    c = lf
    span = 1
    while span < tm:
        c = c + jnp.where(row >= span, pltpu.roll(c, span, axis=0), 0.0)
        span *= 2
    c = c + carry_ref[...]
    carry_ref[...] = c[tm - 1:tm, :]
    qf_ref[0] = (head_norm(yq, gfq_ref[...]) * (HEAD_DIM ** -0.5)).astype(BF16)
    h1, h2, h3 = _split3(c)
    c3 = (h1.astype(F32) + pltpu.roll(h2.astype(F32), N_HEADS, axis=1)
          + pltpu.roll(h3.astype(F32), 2 * N_HEADS, axis=1)
          + jnp.where(lane == 3 * N_HEADS, 1.0, 0.0)).astype(BF16)
    kf_ref[0] = head_norm(yk, gfk_ref[...]).astype(BF16)
    vf_ref[0] = proj(2).astype(BF16)
    qx_ref[0] = _dot(c3, selq_ref[...]).astype(BF16)
    kx_ref[0] = _dot(c3, selk_ref[...]).astype(BF16)

    qm_ref[0] = (rope(head_norm(proj(3), gmq_ref[...])) * (HEAD_DIM ** -0.5)).astype(BF16)
    km = rope(head_norm(proj(4), gmk_ref[...]))
    km_ref[0] = km.astype(BF16)
    for blk in range(tm // MOBA_BLOCK):
        kmean_ref[0, blk] = jnp.sum(km[blk * MOBA_BLOCK:(blk + 1) * MOBA_BLOCK], axis=0,
                                    keepdims=True) * (1.0 / MOBA_BLOCK)
    vm_ref[0] = proj(5).astype(BF16)

    for c in range(2):
        yg = proj(6 + c)
        sg_ref[0, :, c * GROUP_W:(c + 1) * GROUP_W] = yg * jax.nn.sigmoid(yg)


def _l0_in(x, g, wfox, wrest, wf, bfp, gfq, gfk, gmq, gmk, cos, sin, pmat, selq, selk):
    B, S, D = x.shape
    tm = TM_IN
    nblk = S // MOBA_BLOCK
    grid = (B, S // tm)
    row = lambda w: pl.BlockSpec((1, tm, w), lambda b, i: (b, i, 0))
    const = lambda a: pl.BlockSpec(a.shape, lambda b, i: (0,) * a.ndim)
    bf_act = lambda w: jax.ShapeDtypeStruct((B, S, w), BF16)
    out_shape = (
        bf_act(GROUP_W), bf_act(GROUP_W), bf_act(GROUP_W),
        bf_act(LANES), bf_act(LANES),
        bf_act(GROUP_W), bf_act(GROUP_W), bf_act(GROUP_W),
        jax.ShapeDtypeStruct((B, nblk, 1, GROUP_W), F32),
        jax.ShapeDtypeStruct((B, S, 2 * GROUP_W), F32),
    )
    out_specs = (
        row(GROUP_W), row(GROUP_W), row(GROUP_W), row(LANES), row(LANES),
        row(GROUP_W), row(GROUP_W), row(GROUP_W),
        pl.BlockSpec((1, tm // MOBA_BLOCK, 1, GROUP_W), lambda b, i: (b, i, 0, 0)),
        row(2 * GROUP_W),
    )
    in_specs = [
        row(D), const(g), const(wfox), const(wrest), const(wf), const(bfp),
        const(gfq), const(gfk), const(gmq), const(gmk),
        pl.BlockSpec((tm, LANES), lambda b, i: (i, 0)), pl.BlockSpec((tm, LANES), lambda b, i: (i, 0)),
        const(pmat), const(selq), const(selk),
    ]
    return pl.pallas_call(
        _l0_in_kernel,
        grid=grid, in_specs=in_specs, out_specs=out_specs, out_shape=out_shape,
        scratch_shapes=[pltpu.VMEM((1, LANES), F32)],
        compiler_params=pltpu.CompilerParams(
            dimension_semantics=("parallel", "arbitrary"), vmem_limit_bytes=VMEM_LIMIT),
        name="l0_in",
    )(x, g, wfox, wrest, wf, bfp, gfq, gfk, gmq, gmk, cos, sin, pmat, selq, selk)


def _attn_kernel(ti_ref, tj_ref, *refs, moba, online):
    if moba:
        q_ref, k_ref, v_ref, sg_ref, shift_ref, kmean_ref, y_ref, k2_ref, v2_ref, q2_ref, s_ref = refs
    else:
        q_ref, k_ref, v_ref, sg_ref, shift_ref, qx_ref, kx_ref, y_ref, k2_ref, v2_ref, q2_ref, s_ref = refs
    S = k_ref.shape[1]
    nq = S // TQ
    nsteps = nq * (nq + 1) // 2
    assert STEPS_PER_TRIP % 2 == 0 and nsteps % STEPS_PER_TRIP == 0
    pair = pl.program_id(1)
    half = LANES // 2
    ext_w = 2 * N_HEADS
    rows = 512

    zero = jnp.zeros((rows, LANES), BF16)
    one = jnp.ones((rows, LANES), BF16)
    lane = lax.broadcasted_iota(jnp.int32, (rows, LANES), 1)
    lane_b = lane.astype(BF16)
    group_b = (lane // ext_w).astype(BF16)
    for r in range(S // rows):
        sl = slice(r * rows, (r + 1) * rows)
        k = k_ref[0, sl, :]
        v = v_ref[0, sl, :]
        if moba:
            blk_b = ((lax.broadcasted_iota(jnp.int32, (rows, LANES), 0) + r * rows) // MOBA_BLOCK).astype(BF16)
        else:
            kx = kx_ref[0, sl, :]
        for hh in range(2):
            mine = (lane_b >= half) if hh else (lane_b < half)
            k2_ref[hh, sl, 0:LANES] = jnp.where(mine, k, zero)
            if moba:
                ext = jnp.where(lane_b == blk_b + hh * ext_w, one, zero)
            else:
                head_b = jnp.full((rows, LANES), 2 * pair + hh, jnp.int32).astype(BF16)
                ext = jnp.where(group_b == head_b, kx, zero)
            if not online:
                ext = jnp.where(lane_b == SHIFT_LANE, one, ext)
            k2_ref[hh, sl, LANES:2 * LANES] = ext
            v2_ref[hh, sl, :] = jnp.where(mine, v, jnp.where(lane_b == (0 if hh else half), one, zero))

    def q_rows(i):
        return pl.ds(pl.multiple_of(i * TQ, TQ), TQ)

    def kv_rows(j):
        return pl.ds(pl.multiple_of(j * TK, TK), TK)

    if moba:
        km = kmean_ref[0].astype(F32)
        nblk = km.shape[0]
        lane_k = lax.broadcasted_iota(jnp.int32, (nblk, LANES), 1)
        kmt = jnp.concatenate(
            [jnp.where(lane_k < half, km, 0.0), jnp.where(lane_k >= half, km, 0.0),
             jnp.zeros((LANES - 2 * nblk, LANES), F32)], axis=0).astype(BF16)

        def select(i):
            q = q_ref[0, q_rows(i), :]
            st = _dot_nt(kmt, q)[0:2 * nblk]
            n = lax.broadcasted_iota(jnp.int32, (2 * nblk, TQ), 0) % nblk
            own = i * (TQ // MOBA_BLOCK) + lax.broadcasted_iota(jnp.int32, (2 * nblk, TQ), 1) // MOBA_BLOCK
            valid = n < own
            gsc = jnp.where(valid, st, -jnp.inf)
            rank = jnp.zeros((2 * nblk, TQ), jnp.int32)
            for m in range(nblk):
                gm = jnp.concatenate(
                    [jnp.broadcast_to(gsc[m:m + 1], (nblk, TQ)),
                     jnp.broadcast_to(gsc[nblk + m:nblk + m + 1], (nblk, TQ))], axis=0)
                beats = (gm > gsc) | ((gm == gsc) & (m < n))
                rank = rank + beats.astype(jnp.int32)
            sel = (valid & (rank < MOBA_TOPK)) | (n == own)
            sbt = jnp.concatenate([jnp.where(sel, 0.0, MASK_BIAS),
                                   jnp.zeros((LANES - 2 * nblk, TQ), F32)], axis=0)
            qext = sbt.T
            if not online:
                qext = qext + shift_ref[...]
            q2_ref[q_rows(i), 0:LANES] = q
            q2_ref[q_rows(i), LANES:2 * LANES] = qext.astype(BF16)

        def select_some(ii, _):
            for u in range(SELECT_UNROLL):
                select(SELECT_UNROLL * ii + u)
            return 0

        assert nq % SELECT_UNROLL == 0
        lax.fori_loop(0, nq // SELECT_UNROLL, select_some, 0)
    else:
        shift_b = jnp.broadcast_to(shift_ref[...], (rows, LANES)).astype(BF16)
        for r in range(S // rows):
            sl = slice(r * rows, (r + 1) * rows)
            qext = qx_ref[0, sl, :]
            if not online:
                qext = qext + shift_b
            q2_ref[sl, 0:LANES] = q_ref[0, sl, :]
            q2_ref[sl, LANES:2 * LANES] = qext

    col_minus_row = (lax.broadcasted_iota(jnp.int32, (TQ, TK), 1)
                     - lax.broadcasted_iota(jnp.int32, (TQ, TK), 0))
    lane = lax.broadcasted_iota(jnp.int32, (TQ, LANES), 1)

    def logits(t, slot):
        q2 = q2_ref[q_rows(ti_ref[t + 1]), :]
        ks = kv_rows(tj_ref[t + 1])
        for hh in range(2):
            s_ref[slot, hh] = _dot_nt(q2, k2_ref[hh, ks, :])

    def init_state():
        if online:
            return tuple((jnp.full((TQ, 1), -jnp.inf, F32), jnp.zeros((TQ, LANES), F32)) for _ in range(2))
        return tuple(jnp.zeros((TQ, LANES), F32) for _ in range(2))

    def write_out(state, i):
        accs = [st[1] for st in state] if online else state
        l0 = jnp.maximum(accs[0][:, half:half + 1], TINY)
        l1 = jnp.maximum(accs[1][:, 0:1], TINY)
        o = jnp.where(lane < half, accs[0] / l0, accs[1] / l1)
        y_ref[0, q_rows(i), :] = (o * sg_ref[0, q_rows(i), :]).astype(BF16)

    def step(t, slot, state):
        i_prev = ti_ref[t]
        write_out(state, i_prev)
        closed = i_prev == tj_ref[t]
        state = jax.tree.map(lambda a, b: jnp.where(closed, a, b), init_state(), state)

        i = ti_ref[t + 1]
        j = tj_ref[t + 1]
        keep = col_minus_row <= (i - j) * TK
        ks = kv_rows(j)
        new = []
        for hh in range(2):
            s = jnp.where(keep, s_ref[slot, hh], NEG_BIG)
            if online:
                m, acc = state[hh]
                m_new = jnp.maximum(m, jnp.max(s, axis=1, keepdims=True))
                alpha = jnp.exp(m - m_new)
                p = jnp.exp(s - m_new).astype(BF16)
                new.append((m_new, alpha * acc + _dot(p, v2_ref[hh, ks, :])))
            else:
                new.append(state[hh] + _dot(jnp.exp(s).astype(BF16), v2_ref[hh, ks, :]))
        return tuple(new)

    def trip(tt, state):
        for u in range(STEPS_PER_TRIP):
            t = STEPS_PER_TRIP * tt + u
            logits(t + 1, (u + 1) % 2)
            state = step(t, u % 2, state)
        return state

    logits(0, 0)
    state = lax.fori_loop(0, nsteps // STEPS_PER_TRIP, trip, init_state())
    write_out(state, ti_ref[nsteps])


def _attention(q, k, v, sg, shift, extra, *, moba, online):
    B, S, _ = q.shape
    npair = GROUP_W // LANES
    nq = S // TQ
    order = [(0, 1)] + [(i, j) for i in range(nq) for j in range(i + 1)] + [(0, 0), (0, 0)]
    ti = jnp.asarray([i for i, _ in order], jnp.int32)
    tj = jnp.asarray([j for _, j in order], jnp.int32)
    sg_off = npair if moba else 0
    seq = lambda off: pl.BlockSpec((1, S, LANES), lambda b, p, ti, tj: (b, 0, p + off))
    if moba:
        (kmean,) = extra
        extra_specs = [pl.BlockSpec((1, kmean.shape[1], LANES), lambda b, p, ti, tj: (b, 0, p))]
    else:
        all_heads = pl.BlockSpec((1, S, LANES), lambda b, p, ti, tj: (b, 0, 0))
        extra_specs = [all_heads, all_heads]
    return pl.pallas_call(
        functools.partial(_attn_kernel, moba=moba, online=online),
        grid_spec=pltpu.PrefetchScalarGridSpec(
            num_scalar_prefetch=2,
            grid=(B, npair),
            in_specs=[seq(0), seq(0), seq(0), seq(sg_off),
                      pl.BlockSpec((1, LANES), lambda b, p, ti, tj: (0, 0))] + extra_specs,
            out_specs=seq(0),
            scratch_shapes=[pltpu.VMEM((2, S, 2 * LANES), BF16), pltpu.VMEM((2, S, LANES), BF16),
                            pltpu.VMEM((S, 2 * LANES), BF16), pltpu.VMEM((2, 2, TQ, TK), F32)]),
        out_shape=jax.ShapeDtypeStruct((B, S, GROUP_W), BF16),
        compiler_params=pltpu.CompilerParams(
            dimension_semantics=("parallel", "parallel"), vmem_limit_bytes=VMEM_LIMIT),
        name="moba_attn" if moba else "fox_attn",
    )(ti, tj, q, k, v, sg, shift, *extra)


def _l1_kernel(x_ref, yf_ref, ym_ref, wo0_ref, g_ref, w1_ref, cw_ref, cb_ref, lng_ref, lnb_ref,
               wo1_ref, o_ref, ubuf_ref, cbuf_ref):
    tm = x_ref.shape[1]
    C = x_ref.shape[2]
    i = pl.program_id(1)

    @pl.when(i == 0)
    def _():
        ubuf_ref[0:HALO, :] = jnp.zeros((HALO, C), F32)

    base = HALO - (CONV_WIDTH - 1)

    def conv_rows(r0, cs):
        acc = None
        for phase in range(SUBLANES):
            rows = CONV_ROWS + (SUBLANES if phase else 0)
            part = None
            for j in range(CONV_WIDTH):
                if (base + j) % SUBLANES != phase:
                    continue
                term = cw_ref[j:j + 1, cs] * ubuf_ref[pl.ds(r0 + base + j - phase, rows), cs]
                part = term if part is None else part + term
            if phase:
                part = pltpu.roll(part, rows - phase, axis=0)[0:CONV_ROWS]
            acc = part if acc is None else acc + part
        return acc

    nblk = tm // L1_ROWS
    x1s, zs = [], []
    for blk in range(nblk):
        rs = slice(blk * L1_ROWS, (blk + 1) * L1_ROWS)
        y = jnp.concatenate([yf_ref[0, rs, :], ym_ref[0, rs, :]], axis=1)
        x1 = x_ref[0, rs, :] + _dot(y, wo0_ref[...])

        ms = jnp.mean(x1 * x1, axis=-1, keepdims=True)
        h = (x1 * lax.rsqrt(ms + NORM_EPS) * g_ref[...]).astype(BF16)

        val = _dot(h, w1_ref[:, 0:C])
        glu = _dot(h, w1_ref[:, C:2 * C])
        ubuf_ref[HALO + blk * L1_ROWS:HALO + (blk + 1) * L1_ROWS, :] = val * jax.nn.sigmoid(glu)
        x1s.append(x1)
        zs.append(_dot(h, w1_ref[:, 2 * C:3 * C]))

    for blk in range(nblk):
        rs = slice(blk * L1_ROWS, (blk + 1) * L1_ROWS)
        x1, z = x1s[blk], zs[blk]
        for c in range(C // LANES):
            cs = slice(c * LANES, (c + 1) * LANES)
            for r in range(L1_ROWS // CONV_ROWS):
                r0 = blk * L1_ROWS + r * CONV_ROWS
                cbuf_ref[r0:r0 + CONV_ROWS, cs] = conv_rows(r0, cs)

        cv = cbuf_ref[rs, :] + cb_ref[...]
        mu = jnp.mean(cv, axis=-1, keepdims=True)
        xc = cv - mu
        var = jnp.mean(xc * xc, axis=-1, keepdims=True)
        yln = xc * lax.rsqrt(var + LN_EPS) * lng_ref[...] + lnb_ref[...]
        a = (yln * jax.nn.sigmoid(yln)) * (z * jax.nn.sigmoid(z))
        o_ref[0, rs, :] = x1 + _dot(a.astype(BF16), wo1_ref[...])

    ubuf_ref[0:HALO, :] = ubuf_ref[tm:tm + HALO, :]


def _l1_conv(x, yf, ym, wo0, g, w1, cw, cb, lng, lnb, wo1):
    B, S, D = x.shape
    tm = TM_L1
    row = lambda w: pl.BlockSpec((1, tm, w), lambda b, i: (b, i, 0))
    const = lambda a: pl.BlockSpec(a.shape, lambda b, i: (0,) * a.ndim)
    return pl.pallas_call(
        _l1_kernel,
        grid=(B, S // tm),
        in_specs=[row(D), row(GROUP_W), row(GROUP_W), const(wo0), const(g), const(w1), const(cw),
                  const(cb), const(lng), const(lnb), const(wo1)],
        out_specs=row(D),
        out_shape=jax.ShapeDtypeStruct((B, S, D), F32),
        scratch_shapes=[pltpu.VMEM((tm + HALO, D), F32), pltpu.VMEM((tm, D), F32)],
        compiler_params=pltpu.CompilerParams(
            dimension_semantics=("parallel", "arbitrary"), vmem_limit_bytes=VMEM_LIMIT),
        name="l1_conv",
    )(x, yf, ym, wo0, g, w1, cw, cb, lng, lnb, wo1)


def _rope_tables(S):
    half = ROT_DIM // 2
    inv_freq = ROPE_THETA ** (-jnp.arange(half, dtype=F32) / half)
    ang = jnp.arange(S).astype(F32)[:, None] * inv_freq[None, :]
    cos, sin = jnp.cos(ang), jnp.sin(ang)
    ones = jnp.ones((S, HEAD_DIM - ROT_DIM), F32)
    cos_h = jnp.concatenate([cos, cos, ones], axis=1)
    sin_h = jnp.concatenate([-sin, sin, 0.0 * ones], axis=1)
    reps = LANES // HEAD_DIM
    return jnp.tile(cos_h, (1, reps)), jnp.tile(sin_h, (1, reps))


def _forget_spread():
    selq = np.zeros((LANES, LANES), np.float32)
    selk = np.zeros((LANES, LANES), np.float32)
    for hd in range(N_HEADS):
        for piece in range(3):
            selq[piece * N_HEADS + hd, 2 * N_HEADS * hd + piece] = 1.0
            selq[3 * N_HEADS, 2 * N_HEADS * hd + 3 + piece] = 1.0
            selk[3 * N_HEADS, 2 * N_HEADS * hd + piece] = 1.0
            selk[piece * N_HEADS + hd, 2 * N_HEADS * hd + 3 + piece] = -1.0
    return jnp.asarray(selq, BF16), jnp.asarray(selk, BF16)


def kernel(x, l0_norm, l0_w_in, l0_b_f, l0_qn_fox, l0_kn_fox, l0_qn_moba, l0_kn_moba, l0_w_out,
           l1_norm, l1_w_in, l1_conv_w, l1_conv_b, l1_ln_g, l1_ln_b, l1_w_out):
    B, S, D = x.shape
    assert D == D_MODEL and S % MOBA_BLOCK == 0 and S // MOBA_BLOCK <= N_HEADS * 2

    fq = 3 * GROUP_W
    wfox = l0_w_in[:, :fq].astype(BF16)
    wrest = l0_w_in[:, fq + N_HEADS:].astype(BF16)
    wf = jnp.pad(l0_w_in[:, fq:fq + N_HEADS], ((0, 0), (0, LANES - N_HEADS))).astype(BF16)
    bfp = jnp.pad(l0_b_f, (0, LANES - N_HEADS)).reshape(1, LANES)
    tile_gain = lambda gain: jnp.tile(gain, N_HEADS).reshape(1, GROUP_W)
    cos, sin = _rope_tables(S)
    pmat = jnp.asarray(np.kron(np.eye(256 // HEAD_DIM), np.ones((HEAD_DIM, HEAD_DIM))), BF16)
    selq, selk = _forget_spread()

    qf, kf, vf, qx, kx, qm, km, vm, kmean, sg = _l0_in(
        x, l0_norm.reshape(1, D), wfox, wrest, wf, bfp, tile_gain(l0_qn_fox), tile_gain(l0_kn_fox),
        tile_gain(l0_qn_moba), tile_gain(l0_kn_moba), cos, sin, pmat, selq, selk)

    def attend(q, k, v, gq, gk, extra, moba):
        bound = jnp.ceil(1.02 * (HEAD_DIM ** 0.5) * jnp.max(jnp.abs(gq)) * jnp.max(jnp.abs(gk)))
        shift = jnp.zeros((1, LANES), F32).at[0, SHIFT_LANE].set(-bound)
        run = lambda online: functools.partial(_attention, q, k, v, sg, shift, extra, moba=moba, online=online)
        return lax.cond(bound <= MAX_FIXED_SHIFT, run(False), run(True))

    yf = attend(qf, kf, vf, l0_qn_fox, l0_kn_fox, (qx, kx), False)
    ym = attend(qm, km, vm, l0_qn_moba, l0_kn_moba, (kmean.reshape(B, S // MOBA_BLOCK, GROUP_W),), True)

    cw = jnp.pad(l1_conv_w.reshape(CONV_WIDTH, D), ((0, HALO - CONV_WIDTH), (0, 0)))
    return _l1_conv(
        x, yf, ym, l0_w_out.astype(BF16), l1_norm.reshape(1, D), l1_w_in.astype(BF16), cw,
        l1_conv_b.reshape(1, D), l1_ln_g.reshape(1, D), l1_ln_b.reshape(1, D), l1_w_out.astype(BF16))
```
